```python
import jax, jax.numpy as jnp
from jax import lax
import numpy as np

D_MODEL = 1024
BATCH = 4
SEQ = 4096
DEPTH = 4

HEAD_DIM = 64
GROUP_HEADS = 4
GROUP_WIDTH = GROUP_HEADS * HEAD_DIM
N_GROUPS = 4
MIX_WIDTH = N_GROUPS * GROUP_WIDTH
BLOCK = 128
NEG_INF = -1e30

FOX_H = GROUP_HEADS
FOX_BIAS_INIT = 3.0
MLA_H = GROUP_HEADS
MLA_Q_RANK = D_MODEL // 4
MLA_KV_RANK = D_MODEL // 8
MLA_NOPE = HEAD_DIM
MLA_ROPE = HEAD_DIM // 2
MLA_V = HEAD_DIM
ROPE_THETA = 10000.0
SB_H = GROUP_HEADS
SWA_H = GROUP_HEADS
SWA_KV_H = 2
WINDOW = 128
D_FF = ((8 * D_MODEL + 767) // 768) * 256
ALPHA = (2.0 * DEPTH) ** 0.25
BETA = (8.0 * DEPTH) ** -0.25

SPLIT_SIZES = (
    FOX_H * HEAD_DIM, FOX_H * HEAD_DIM, FOX_H * HEAD_DIM, FOX_H,
    MLA_Q_RANK, MLA_KV_RANK, MLA_ROPE,
    SB_H * HEAD_DIM, SB_H * HEAD_DIM, SB_H * HEAD_DIM,
    SWA_H * HEAD_DIM, SWA_KV_H * HEAD_DIM, SWA_KV_H * HEAD_DIM,
)
IN_WIDTH = sum(SPLIT_SIZES)

kernel_name = 'hybrid_fox_mla_stickbreak_swa_deepnorm'


def _layernorm(x, g, b, eps=1e-5):
    xf = x.astype(jnp.float32)
    mu = jnp.mean(xf, axis=-1, keepdims=True)
    var = jnp.mean(jnp.square(xf - mu), axis=-1, keepdims=True)
    return ((xf - mu) * lax.rsqrt(var + eps) * g + b).astype(x.dtype)


def _rmsnorm(x, g, eps=1e-6):
    xf = x.astype(jnp.float32)
    return (xf * lax.rsqrt(jnp.mean(jnp.square(xf), axis=-1, keepdims=True) + eps) * g).astype(x.dtype)


def _group_rmsnorm(mix, g, eps=1e-6):
    B, S, _ = mix.shape
    xf = mix.astype(jnp.float32).reshape(B, S, N_GROUPS, GROUP_WIDTH)
    xf = xf * lax.rsqrt(jnp.mean(jnp.square(xf), axis=-1, keepdims=True) + eps)
    return (xf.reshape(B, S, MIX_WIDTH) * g).astype(mix.dtype)


def _heads(t, n):
    B, S, _ = t.shape
    return t.reshape(B, S, n, -1)


def _rope_tables(S):
    pos = jnp.arange(S, dtype=jnp.float32)
    inv = ROPE_THETA ** (-jnp.arange(0, MLA_ROPE, 2, dtype=jnp.float32) / MLA_ROPE)
    ang = pos[:, None] * inv[None, :]
    return jnp.cos(ang), jnp.sin(ang)


def _rope(x, cos, sin):
    x1, x2 = jnp.split(x.astype(jnp.float32), 2, axis=-1)
    c = cos[None, :, None, :]
    s = sin[None, :, None, :]
    return jnp.concatenate([x1 * c - x2 * s, x1 * s + x2 * c], axis=-1).astype(x.dtype)


def _alibi_slopes(n):
    return jnp.exp2(-8.0 * jnp.arange(1, n + 1, dtype=jnp.float32) / n)


def _causal_softmax_blocks(q, k, v, scale, cum_log_f=None):
    B, S, H, _ = q.shape
    nb = S // BLOCK
    qb = q.reshape(B, nb, BLOCK, H, -1).swapaxes(0, 1)
    kpos = jnp.arange(S)

    def one_block(args):
        i, qi = args
        s = jnp.einsum('bqhd,bkhd->bhqk', qi, k).astype(jnp.float32) * scale
        qpos = i * BLOCK + jnp.arange(BLOCK)
        if cum_log_f is not None:
            cq = lax.dynamic_slice_in_dim(cum_log_f, i * BLOCK, BLOCK, axis=2)
            s = s + cq[..., :, None] - cum_log_f[..., None, :]
        s = jnp.where(kpos[None, :] <= qpos[:, None], s, NEG_INF)
        p = jax.nn.softmax(s, axis=-1).astype(v.dtype)
        return jnp.einsum('bhqk,bkhd->bqhd', p, v)

    out = lax.map(one_block, (jnp.arange(nb), qb))
    return out.swapaxes(0, 1).reshape(B, S, H, -1)


def _stick_breaking_blocks(q, k, v):
    B, S, H, D = q.shape
    nb = S // BLOCK
    scale = D ** -0.5
    qb = q.reshape(B, nb, BLOCK, H, D).swapaxes(0, 1)
    kpos = jnp.arange(S)

    def one_block(args):
        i, qi = args
        z = jnp.einsum('bqhd,bkhd->bhqk', qi, k).astype(jnp.float32) * scale
        qpos = i * BLOCK + jnp.arange(BLOCK)
        strict = kpos[None, :] < qpos[:, None]
        log_1mb = jnp.where(strict, jax.nn.log_sigmoid(-z), 0.0)
        between = lax.cumsum(log_1mb, axis=3, reverse=True) - log_1mb
        a = jnp.where(strict, jnp.exp(jax.nn.log_sigmoid(z) + between), 0.0)
        return jnp.einsum('bhqk,bkhd->bqhd', a.astype(v.dtype), v)

    out = lax.map(one_block, (jnp.arange(nb), qb))
    return out.swapaxes(0, 1).reshape(B, S, H, D)


def _mla(cq, ckv, kr, g_q, g_kv, w_uq, w_ukv, cos, sin):
    B, S, _ = cq.shape
    q = (_rmsnorm(cq, g_q) @ w_uq).reshape(B, S, MLA_H, MLA_NOPE + MLA_ROPE)
    kv = (_rmsnorm(ckv, g_kv) @ w_ukv).reshape(B, S, MLA_H, MLA_NOPE + MLA_V)
    k_nope, v = kv[..., :MLA_NOPE], kv[..., MLA_NOPE:]
    q = jnp.concatenate([q[..., :MLA_NOPE], _rope(q[..., MLA_NOPE:], cos, sin)], axis=-1)
    k_rope = jnp.broadcast_to(_rope(kr[:, :, None, :], cos, sin), (B, S, MLA_H, MLA_ROPE))
    k = jnp.concatenate([k_nope, k_rope.astype(k_nope.dtype)], axis=-1)
    return _causal_softmax_blocks(q.astype(k.dtype), k, v, (MLA_NOPE + MLA_ROPE) ** -0.5)


def _swa_sink_alibi(q, k, v, sinks, slopes):
    B, S, Hq, D = q.shape
    Hkv = k.shape[2]
    G = Hq // Hkv
    nb = S // BLOCK
    qb = q.reshape(B, nb, BLOCK, Hkv, G, D)
    pad = jnp.zeros((B, BLOCK, Hkv, D), k.dtype)
    kp = jnp.concatenate([pad, k], axis=1).reshape(B, nb + 1, BLOCK, Hkv, D)
    vp = jnp.concatenate([pad.astype(v.dtype), v], axis=1).reshape(B, nb + 1, BLOCK, Hkv, D)
    kb = jnp.concatenate([kp[:, :-1], kp[:, 1:]], axis=2)
    vb = jnp.concatenate([vp[:, :-1], vp[:, 1:]], axis=2)
    s = jnp.einsum('bnqhgd,bnkhd->bnhgqk', qb, kb).astype(jnp.float32) * (D ** -0.5)
    dist = jnp.arange(BLOCK)[:, None] + BLOCK - jnp.arange(2 * BLOCK)[None, :]
    band = (dist >= 0) & (dist < WINDOW)
    kpos = (jnp.arange(nb)[:, None] - 1) * BLOCK + jnp.arange(2 * BLOCK)[None, :]
    valid = band[None, :, :] & (kpos >= 0)[:, None, :]
    s = s - slopes.reshape(Hkv, G)[:, :, None, None] * dist.astype(jnp.float32)
    s = jnp.where(valid[None, :, None, None], s, NEG_INF)
    sink = jnp.broadcast_to(sinks.astype(jnp.float32).reshape(Hkv, G)[None, None, :, :, None, None],
                            s.shape[:-1] + (1,))
    p = jax.nn.softmax(jnp.concatenate([s, sink], axis=-1), axis=-1)[..., :-1]
    out = jnp.einsum('bnhgqk,bnkhd->bnqhgd', p.astype(v.dtype), vb)
    return out.reshape(B, S, Hq, D)


def setup_inputs(seed: int = 0) -> dict:
    key = jax.random.key(seed)
    ks = jax.random.split(key, 17)
    L = DEPTH

    def nrm(k, shape, scale):
        return jax.random.normal(k, shape, jnp.float32) * scale

    def gain(k, shape):
        return 1.0 + 0.02 * jax.random.normal(k, shape, jnp.float32)

    return {
        'x': nrm(ks[0], (BATCH, SEQ, D_MODEL), 1.0),
        'w_in': nrm(ks[1], (L, D_MODEL, IN_WIDTH), D_MODEL ** -0.5),
        'fox_b_f': FOX_BIAS_INIT + nrm(ks[2], (L, FOX_H), 0.1),
        'mla_g_q': gain(ks[3], (L, MLA_Q_RANK)),
        'mla_g_kv': gain(ks[4], (L, MLA_KV_RANK)),
        'mla_w_uq': nrm(ks[5], (L, MLA_Q_RANK, MLA_H * (MLA_NOPE + MLA_ROPE)), MLA_Q_RANK ** -0.5),
        'mla_w_ukv': nrm(ks[6], (L, MLA_KV_RANK, MLA_H * (MLA_NOPE + MLA_V)), MLA_KV_RANK ** -0.5),
        'swa_sinks': nrm(ks[7], (L, SWA_H), 0.5),
        'mix_g': gain(ks[8], (L, MIX_WIDTH)),
        'w_o': nrm(ks[9], (L, MIX_WIDTH, D_MODEL), BETA * MIX_WIDTH ** -0.5),
        'ln1_g': gain(ks[10], (L, D_MODEL)),
        'ln1_b': nrm(ks[11], (L, D_MODEL), 0.02),
        'w_gate': nrm(ks[12], (L, D_MODEL, D_FF), D_MODEL ** -0.5),
        'w_up': nrm(ks[13], (L, D_MODEL, D_FF), D_MODEL ** -0.5),
        'w_down': nrm(ks[14], (L, D_FF, D_MODEL), BETA * D_FF ** -0.5),
        'ln2_g': gain(ks[15], (L, D_MODEL)),
        'ln2_b': nrm(ks[16], (L, D_MODEL), 0.02),
    }


def reference(x, w_in, fox_b_f, mla_g_q, mla_g_kv, mla_w_uq, mla_w_ukv, swa_sinks, mix_g, w_o,
              ln1_g, ln1_b, w_gate, w_up, w_down, ln2_g, ln2_b):
    B, S, _ = x.shape
    cos, sin = _rope_tables(S)
    slopes = _alibi_slopes(SWA_H)
    points = [int(p) for p in np.cumsum(SPLIT_SIZES)[:-1]]
    for l in range(DEPTH):
        h = x @ w_in[l]
        (fq, fk, fv, fgate, cq, ckv, kr, sq, sk, sv, wq, wk, wv) = jnp.split(h, points, axis=-1)
        log_f = jax.nn.log_sigmoid(fgate.astype(jnp.float32) + fox_b_f[l].astype(jnp.float32))
        cum = jnp.cumsum(log_f, axis=1).transpose(0, 2, 1)
        out_a = _causal_softmax_blocks(_heads(fq, FOX_H), _heads(fk, FOX_H), _heads(fv, FOX_H),
                                       HEAD_DIM ** -0.5, cum)
        out_b = _mla(cq, ckv, kr, mla_g_q[l], mla_g_kv[l], mla_w_uq[l], mla_w_ukv[l], cos, sin)
        out_c = _stick_breaking_blocks(_heads(sq, SB_H), _heads(sk, SB_H), _heads(sv, SB_H))
        out_d = _swa_sink_alibi(_heads(wq, SWA_H), _heads(wk, SWA_KV_H), _heads(wv, SWA_KV_H),
                                swa_sinks[l], slopes)
        mix = jnp.concatenate([out_a.reshape(B, S, GROUP_WIDTH), out_b.reshape(B, S, GROUP_WIDTH),
                               out_c.reshape(B, S, GROUP_WIDTH), out_d.reshape(B, S, GROUP_WIDTH)],
                              axis=-1).astype(x.dtype)
        y = _group_rmsnorm(mix, mix_g[l]) @ w_o[l]
        x = _layernorm(ALPHA * x + y, ln1_g[l], ln1_b[l])
        f = (jax.nn.silu(x @ w_gate[l]) * (x @ w_up[l])) @ w_down[l]
        x = _layernorm(ALPHA * x + f, ln2_g[l], ln2_b[l])
    return x
```

```python
import functools

import numpy as np
import jax
import jax.numpy as jnp
from jax import lax
from jax.experimental import pallas as pl
from jax.experimental.pallas import tpu as pltpu

F32 = jnp.float32
BF16 = jnp.bfloat16

D_MODEL = 1024
HEAD_DIM = 64
GROUP_HEADS = 4
GROUP_WIDTH = GROUP_HEADS * HEAD_DIM
MLA_Q_RANK = 256
MLA_KV_RANK = 128
MLA_ROPE = 32
SWA_WINDOW = 128
D_FF = 2816
NEG_INF = -1e30
DEPTH = 4
ALPHA = (2.0 * DEPTH) ** 0.25
ROPE_THETA = 10000.0

LANES = 128
VMEM_LIMIT = 52 * 1024 * 1024

C_FQ, C_FK, C_FV = 0, 256, 512
C_CQ, C_CKV = 768, 1024
C_SQ, C_SK, C_SV = 1152, 1408, 1664
C_WQ, C_WK, C_WV = 1920, 2176, 2304
C_KR1, C_KR2, C_GATE = 2432, 2560, 2688
IN_PERM_WIDTH = 2816

TM_IN = 512
TM_OUT = 512
TQ = 256
TQ_SWA = 512
F_CHUNK = 256


def _nt_dot(a, b):
    return lax.dot_general(a, b, (((1,), (1,)), ((), ())), preferred_element_type=F32)


def _dot(a, b):
    return jnp.dot(a, b, preferred_element_type=F32)


def _log_sigmoid(x):
    return jnp.minimum(x, 0.0) - jnp.log(1.0 + jnp.exp(-jnp.abs(x)))


def _in_proj_columns():
    sizes = (256, 256, 256, 4, MLA_Q_RANK, MLA_KV_RANK, MLA_ROPE, 256, 256, 256, 256, 128, 128)
    starts = np.concatenate([[0], np.cumsum(sizes)[:-1]])
    (fq, fk, fv, fg, cq, ckv, kr, sq, sk, sv, wq, wk, wv) = [int(s) for s in starts]
    idx = np.full((IN_PERM_WIDTH,), -1, np.int64)

    def put(dst, src, n):
        idx[dst:dst + n] = np.arange(src, src + n)

    put(C_FQ, fq, 256); put(C_FK, fk, 256); put(C_FV, fv, 256)
    put(C_CQ, cq, MLA_Q_RANK); put(C_CKV, ckv, MLA_KV_RANK)
    put(C_SQ, sq, 256); put(C_SK, sk, 256); put(C_SV, sv, 256)
    for slot, head in enumerate((0, 2, 1, 3)):
        put(C_WQ + 64 * slot, wq + 64 * head, 64)
    put(C_WK, wk, 128); put(C_WV, wv, 128)
    half = MLA_ROPE // 2
    put(C_KR1 + 64, kr, MLA_ROPE)
    put(C_KR2 + 64, kr + half, half)
    put(C_KR2 + 64 + half, kr, half)
    put(C_GATE, fg, 4)
    return idx


def _gather_cols(w, idx):
    safe = np.where(idx < 0, 0, idx)
    out = jnp.take(w, jnp.asarray(safe, jnp.int32), axis=-1)
    return jnp.where(jnp.asarray(idx >= 0), out, 0.0)


def _mla_q_columns():
    half = MLA_ROPE // 2
    per = HEAD_DIM + MLA_ROPE
    idx = np.full((2 * GROUP_HEADS * LANES,), -1, np.int64)
    for h in range(GROUP_HEADS):
        idx[LANES * h:LANES * h + per] = np.arange(per * h, per * h + per)
        base = GROUP_HEADS * LANES + LANES * h + HEAD_DIM
        idx[base:base + half] = np.arange(per * h + HEAD_DIM + half, per * h + per)
        idx[base + half:base + 2 * half] = np.arange(per * h + HEAD_DIM, per * h + HEAD_DIM + half)
    return idx


def _mla_kv_columns():
    idx = np.full((GROUP_HEADS * LANES + GROUP_WIDTH,), -1, np.int64)
    for h in range(GROUP_HEADS):
        idx[LANES * h:LANES * h + HEAD_DIM] = np.arange(2 * HEAD_DIM * h, 2 * HEAD_DIM * h + HEAD_DIM)
        v0 = GROUP_HEADS * LANES + HEAD_DIM * h
        idx[v0:v0 + HEAD_DIM] = np.arange(2 * HEAD_DIM * h + HEAD_DIM, 2 * HEAD_DIM * (h + 1))
    return idx


def _mix_row_order():
    idx = np.arange(4 * GROUP_WIDTH)
    base = 3 * GROUP_WIDTH
    for slot, head in enumerate((0, 2, 1, 3)):
        idx[base + 64 * slot:base + 64 * (slot + 1)] = np.arange(base + 64 * head, base + 64 * (head + 1))
    return idx


def _rope_slot_tables(S):
    pos = jnp.arange(S, dtype=F32)
    inv = ROPE_THETA ** (-jnp.arange(0, MLA_ROPE, 2, dtype=F32) / MLA_ROPE)
    ang = pos[:, None] * inv[None, :]
    cos, sin = jnp.cos(ang), jnp.sin(ang)
    ones = jnp.ones((S, HEAD_DIM), F32)
    zeros32 = jnp.zeros((S, LANES - HEAD_DIM - MLA_ROPE), F32)
    tc = jnp.concatenate([ones, cos, cos, zeros32], axis=1)
    ts = jnp.concatenate([jnp.zeros((S, HEAD_DIM), F32), -sin, sin, zeros32], axis=1)
    return tc, ts


def _in_proj_kernel(x_ref, w_ref, wq_ref, wkv_ref, gq_ref, gkv_ref, bf_ref, tc_ref, ts_ref, tri_ref,
                    qa_ref, ka_ref, va_ref, cum_ref, qb_ref, kb_ref, vb_ref,
                    qc_ref, kc_ref, vc_ref, qd_ref, kd_ref, vd_ref, carry_ref,
                    *, tiles_per_seq, mla_scale):
    tm = x_ref.shape[0]
    i = pl.program_id(0)
    xb = x_ref[...].astype(BF16)

    def proj(c0, n):
        return _dot(xb, w_ref[:, c0:c0 + n])

    lo_half = lax.broadcasted_iota(jnp.int32, (tm, LANES), 1) < HEAD_DIM

    def store_slots(ref, packed, scale):
        for p in range(2):
            pair = packed[:, LANES * p:LANES * (p + 1)] * scale
            ref[:, 2 * LANES * p:2 * LANES * p + LANES] = jnp.where(lo_half, pair, 0.0).astype(BF16)
            ref[:, 2 * LANES * p + LANES:2 * LANES * (p + 1)] = jnp.where(lo_half, 0.0, pair).astype(BF16)

    head_scale = HEAD_DIM ** -0.5
    store_slots(qa_ref, proj(C_FQ, 256), head_scale)
    ka_ref[...] = proj(C_FK, 256).astype(BF16)
    va_ref[...] = proj(C_FV, 256).astype(BF16)
    store_slots(qc_ref, proj(C_SQ, 256), head_scale)
    kc_ref[...] = proj(C_SK, 256).astype(BF16)
    vc_ref[...] = proj(C_SV, 256).astype(BF16)
    store_slots(qd_ref, proj(C_WQ, 256), head_scale)
    kd_ref[...] = proj(C_WK, 128).astype(BF16)
    vd_ref[...] = proj(C_WV, 128).astype(BF16)

    tc = tc_ref[...]
    ts = ts_ref[...]
    cq = proj(C_CQ, MLA_Q_RANK)
    cqn = (cq * lax.rsqrt(jnp.mean(cq * cq, axis=-1, keepdims=True) + 1e-6) * gq_ref[...]).astype(BF16)
    q12 = _dot(cqn, wq_ref[...])
    for h in range(GROUP_HEADS):
        q1 = q12[:, LANES * h:LANES * (h + 1)]
        q2 = q12[:, LANES * (GROUP_HEADS + h):LANES * (GROUP_HEADS + h + 1)]
        qb_ref[:, LANES * h:LANES * (h + 1)] = ((q1 * tc + q2 * ts) * mla_scale).astype(BF16)
    ckv = proj(C_CKV, MLA_KV_RANK)
    ckvn = (ckv * lax.rsqrt(jnp.mean(ckv * ckv, axis=-1, keepdims=True) + 1e-6) * gkv_ref[...]).astype(BF16)
    kv = _dot(ckvn, wkv_ref[...])
    k_rope = proj(C_KR1, LANES) * tc + proj(C_KR2, LANES) * ts
    for h in range(GROUP_HEADS):
        kb_ref[:, LANES * h:LANES * (h + 1)] = (kv[:, LANES * h:LANES * (h + 1)] + k_rope).astype(BF16)
    vb_ref[...] = kv[:, GROUP_HEADS * LANES:].astype(BF16)

    log_f = _log_sigmoid(proj(C_GATE, LANES) + bf_ref[...])
    lf = jnp.transpose(log_f)[0:8, :]
    hi = lf.astype(BF16)
    lo = (lf - hi.astype(F32)).astype(BF16)
    tri = tri_ref[...]
    local = _dot(hi, tri) + _dot(lo, tri)
    carry = jnp.where(i % tiles_per_seq == 0, 0.0, carry_ref[:, 0:1])
    cum = local + carry
    carry_ref[...] = jnp.broadcast_to(cum[:, tm - 1:tm], carry_ref.shape)
    tk = cum_ref.shape[-1]
    for r in range(tm // tk):
        cum_ref[0, r] = cum[:, tk * r:tk * (r + 1)]


def _in_proj(x2, w_in_p, wq_p, wkv_p, g_q, g_kv, b_f, tc, ts, tri, *, B, S):
    M = x2.shape[0]
    tm = TM_IN
    tiles_per_seq = S // tm
    row = lambda i: (i, 0)
    const = lambda i: (0, 0)
    pos = lambda i: (i % tiles_per_seq, 0)
    bf = lambda w: jax.ShapeDtypeStruct((M, w), BF16)
    out_shape = (bf(512), bf(256), bf(256),
                 jax.ShapeDtypeStruct((B, S // TQ, 8, TQ), F32),
                 bf(512), bf(512), bf(256),
                 bf(512), bf(256), bf(256),
                 bf(512), bf(128), bf(128))
    ob = lambda w: pl.BlockSpec((tm, w), row)
    out_specs = (ob(512), ob(256), ob(256),
                 pl.BlockSpec((1, tm // TQ, 8, TQ), lambda i: (i // tiles_per_seq, i % tiles_per_seq, 0, 0)),
                 ob(512), ob(512), ob(256),
                 ob(512), ob(256), ob(256),
                 ob(512), ob(128), ob(128))
    in_specs = [
        pl.BlockSpec((tm, D_MODEL), row),
        pl.BlockSpec((D_MODEL, IN_PERM_WIDTH), const),
        pl.BlockSpec(wq_p.shape, const),
        pl.BlockSpec(wkv_p.shape, const),
        pl.BlockSpec((1, MLA_Q_RANK), const),
        pl.BlockSpec((1, MLA_KV_RANK), const),
        pl.BlockSpec((1, LANES), const),
        pl.BlockSpec((tm, LANES), pos),
        pl.BlockSpec((tm, LANES), pos),
        pl.BlockSpec((tm, tm), const),
    ]
    kern = functools.partial(_in_proj_kernel, tiles_per_seq=tiles_per_seq,
                             mla_scale=float((HEAD_DIM + MLA_ROPE) ** -0.5))
    return pl.pallas_call(
        kern, out_shape=out_shape, grid=(M // tm,), in_specs=in_specs, out_specs=out_specs,
        scratch_shapes=[pltpu.VMEM((8, LANES), F32)],
        compiler_params=pltpu.CompilerParams(dimension_semantics=("arbitrary",),
                                             vmem_limit_bytes=VMEM_LIMIT),
        name="in_proj",
    )(x2, w_in_p, wq_p, wkv_p, g_q, g_kv, b_f, tc, ts, tri)


def _pair_norm_store(o_ref, heads, g_ref, rows=None):
    t = heads[0].shape[0]
    lo_half = lax.broadcasted_iota(jnp.int32, (t, LANES), 1) < HEAD_DIM
    pairs = [jnp.where(lo_half, heads[2 * p], heads[2 * p + 1]) for p in range(2)]
    ss = sum(jnp.sum(p * p, axis=-1, keepdims=True) for p in pairs)
    inv = lax.rsqrt(ss * (1.0 / GROUP_WIDTH) + 1e-6)
    for p in range(2):
        val = (pairs[p] * inv * g_ref[:, LANES * p:LANES * (p + 1)]).astype(o_ref.dtype)
        if rows is None:
            o_ref[:, LANES * p:LANES * (p + 1)] = val
        else:
            o_ref[rows, LANES * p:LANES * (p + 1)] = val


def _softmax_attn_kernel(*refs, k_cols, has_bias):
    if has_bias:
        q_ref, k_ref, v_ref, cum_ref, g_ref, o_ref = refs
    else:
        q_ref, k_ref, v_ref, g_ref, o_ref = refs
        cum_ref = None
    tq = q_ref.shape[0]
    tk = tq
    i = pl.program_id(1)
    causal = (lax.broadcasted_iota(jnp.int32, (tq, tk), 1)
              <= lax.broadcasted_iota(jnp.int32, (tq, tk), 0))
    heads = []
    for h in range(GROUP_HEADS):
        qh = q_ref[:, LANES * h:LANES * (h + 1)]
        kc = k_cols[h]
        vc = LANES * (h // 2)

        def step(kb, carry, masked, h=h, qh=qh, kc=kc, vc=vc):
            m, l, acc = carry
            r0 = pl.multiple_of(kb * tk, tk)
            s = _nt_dot(qh, k_ref[pl.ds(r0, tk), kc:kc + LANES])
            if has_bias:
                s = s - cum_ref[0, kb][h:h + 1, :]
            if masked:
                s = jnp.where(causal, s, NEG_INF)
            m_new = jnp.maximum(m, jnp.max(s, axis=-1, keepdims=True))
            alpha = jnp.exp(m - m_new)
            p = jnp.exp(s - m_new)
            l = alpha * l + jnp.sum(p, axis=-1, keepdims=True)
            acc = alpha * acc + _dot(p.astype(BF16), v_ref[pl.ds(r0, tk), vc:vc + LANES])
            return m_new, l, acc

        init = (jnp.full((tq, 1), NEG_INF, F32), jnp.zeros((tq, 1), F32), jnp.zeros((tq, LANES), F32))
        carry = lax.fori_loop(0, i, lambda kb, c: step(kb, c, False), init)
        _, l, acc = step(i, carry, True)
        heads.append(acc / l)
    _pair_norm_store(o_ref, heads, g_ref)


def _softmax_attn(q, k, v, cum, g, *, B, S, k_cols, name):
    M = q.shape[0]
    nq = S // TQ
    kw = k.shape[1]
    has_bias = cum is not None
    in_specs = [
        pl.BlockSpec((TQ, 4 * LANES), lambda b, i: (b * nq + i, 0)),
        pl.BlockSpec((S, kw), lambda b, i: (b, 0)),
        pl.BlockSpec((S, GROUP_WIDTH), lambda b, i: (b, 0)),
    ]
    args = [q, k, v]
    if has_bias:
        in_specs.append(pl.BlockSpec((1, nq, 8, TQ), lambda b, i: (b, 0, 0, 0)))
        args.append(cum)
    in_specs.append(pl.BlockSpec((1, GROUP_WIDTH), lambda b, i: (0, 0)))
    args.append(g)
    kern = functools.partial(_softmax_attn_kernel, k_cols=k_cols, has_bias=has_bias)
    return pl.pallas_call(
        kern, out_shape=jax.ShapeDtypeStruct((M, GROUP_WIDTH), BF16), grid=(B, nq),
        in_specs=in_specs,
        out_specs=pl.BlockSpec((TQ, GROUP_WIDTH), lambda b, i: (b * nq + i, 0)),
        compiler_params=pltpu.CompilerParams(dimension_semantics=("arbitrary", "arbitrary"),
                                             vmem_limit_bytes=VMEM_LIMIT),
        name=name,
    )(*args)


def _stick_kernel(q_ref, k_ref, v_ref, tri_ref, g_ref, o_ref):
    tq = q_ref.shape[0]
    tk = tq
    i = pl.program_id(1)
    strict = (lax.broadcasted_iota(jnp.int32, (tq, tk), 1)
              < lax.broadcasted_iota(jnp.int32, (tq, tk), 0))
    tri = tri_ref[...]
    heads = []
    for h in range(GROUP_HEADS):
        qh = q_ref[:, LANES * h:LANES * (h + 1)]
        c0 = LANES * (h // 2)

        def block(kb, rest, masked, qh=qh, c0=c0):
            r0 = pl.multiple_of(kb * tk, tk)
            z = _nt_dot(qh, k_ref[pl.ds(r0, tk), c0:c0 + LANES])
            log_1mb = -(jnp.maximum(z, 0.0) + jnp.log(1.0 + jnp.exp(-jnp.abs(z))))
            if masked:
                log_1mb = jnp.where(strict, log_1mb, 0.0)
            hi = log_1mb.astype(BF16)
            lo = (log_1mb - hi.astype(F32)).astype(BF16)
            between = _dot(hi, tri) + _dot(lo, tri) + rest
            a = jnp.exp(z + log_1mb + between)
            if masked:
                a = jnp.where(strict, a, 0.0)
            pv = _dot(a.astype(BF16), v_ref[pl.ds(r0, tk), c0:c0 + LANES])
            return pv, rest + jnp.sum(log_1mb, axis=-1, keepdims=True)

        acc, rest = block(i, jnp.zeros((tq, 1), F32), True)

        def body(t, carry):
            acc, rest = carry
            pv, rest = block(i - 1 - t, rest, False)
            return acc + pv, rest

        acc, _ = lax.fori_loop(0, i, body, (acc, rest))
        heads.append(acc)
    _pair_norm_store(o_ref, heads, g_ref)


def _stick_attn(q, k, v, tri, g, *, B, S):
    M = q.shape[0]
    nq = S // TQ
    return pl.pallas_call(
        _stick_kernel, out_shape=jax.ShapeDtypeStruct((M, GROUP_WIDTH), BF16), grid=(B, nq),
        in_specs=[
            pl.BlockSpec((TQ, 4 * LANES), lambda b, i: (b * nq + i, 0)),
            pl.BlockSpec((S, GROUP_WIDTH), lambda b, i: (b, 0)),
            pl.BlockSpec((S, GROUP_WIDTH), lambda b, i: (b, 0)),
            pl.BlockSpec((TQ, TQ), lambda b, i: (0, 0)),
            pl.BlockSpec((1, GROUP_WIDTH), lambda b, i: (0, 0)),
        ],
        out_specs=pl.BlockSpec((TQ, GROUP_WIDTH), lambda b, i: (b * nq + i, 0)),
        compiler_params=pltpu.CompilerParams(dimension_semantics=("arbitrary", "arbitrary"),
                                             vmem_limit_bytes=VMEM_LIMIT),
        name="stick_attn",
    )(q, k, v, tri, g)


def _swa_kernel(sink_ref, q_ref, kc_ref, kp_ref, vc_ref, vp_ref, g_ref, o_ref, *, slopes):
    tq = q_ref.shape[0]
    w = SWA_WINDOW
    i = pl.program_id(1)
    row = lax.broadcasted_iota(jnp.int32, (w, w), 0)
    col = lax.broadcasted_iota(jnp.int32, (w, w), 1)
    in_cur = col <= row
    in_prev = col > row
    dist_cur = (row - col).astype(F32)
    dist_prev = (row + w - col).astype(F32)
    for r in range(tq // w):
        rows = slice(w * r, w * (r + 1))
        k_cur = kc_ref[rows, :]
        v_cur = vc_ref[rows, :]
        if r == 0:
            k_prev, v_prev = kp_ref[...], vp_ref[...]
        else:
            k_prev, v_prev = kc_ref[w * (r - 1):w * r, :], vc_ref[w * (r - 1):w * r, :]
        heads = []
        for slot in range(GROUP_HEADS):
            head = slot // 2 + 2 * (slot % 2)
            qh = q_ref[rows, LANES * slot:LANES * (slot + 1)]
            sink = sink_ref[head]
            s_cur = jnp.where(in_cur, _nt_dot(qh, k_cur) - slopes[head] * dist_cur, NEG_INF)
            s_prev = jnp.where(in_prev, _nt_dot(qh, k_prev) - slopes[head] * dist_prev, NEG_INF)
            if r == 0:
                s_prev = jnp.where(i > 0, s_prev, NEG_INF)
            m = jnp.maximum(jnp.maximum(jnp.max(s_cur, axis=-1, keepdims=True),
                                        jnp.max(s_prev, axis=-1, keepdims=True)), sink)
            p_cur = jnp.exp(s_cur - m)
            p_prev = jnp.exp(s_prev - m)
            den = (jnp.sum(p_cur, axis=-1, keepdims=True) + jnp.sum(p_prev, axis=-1, keepdims=True)
                   + jnp.exp(sink - m))
            pv = _dot(p_cur.astype(BF16), v_cur) + _dot(p_prev.astype(BF16), v_prev)
            heads.append(pv / den)
        _pair_norm_store(o_ref, heads, g_ref, rows=rows)


def _swa_attn(sinks, q, k, v, g, *, B, S):
    M = q.shape[0]
    tq = TQ_SWA
    nq = S // tq
    per = tq // SWA_WINDOW
    slopes = tuple(float(2.0 ** (-8.0 * (h + 1) / GROUP_HEADS)) for h in range(GROUP_HEADS))
    cur = lambda b, i: (b * nq + i, 0)
    prev = lambda b, i: (b * nq * per + jnp.maximum(i * per - 1, 0), 0)
    kern = functools.partial(_swa_kernel, slopes=slopes)
    return pl.pallas_call(
        kern, out_shape=jax.ShapeDtypeStruct((M, GROUP_WIDTH), BF16), grid=(B, nq),
        in_specs=[
            pl.BlockSpec(memory_space=pltpu.SMEM),
            pl.BlockSpec((tq, 4 * LANES), cur),
            pl.BlockSpec((tq, LANES), cur),
            pl.BlockSpec((SWA_WINDOW, LANES), prev),
            pl.BlockSpec((tq, LANES), cur),
            pl.BlockSpec((SWA_WINDOW, LANES), prev),
            pl.BlockSpec((1, GROUP_WIDTH), lambda b, i: (0, 0)),
        ],
        out_specs=pl.BlockSpec((tq, GROUP_WIDTH), cur),
        compiler_params=pltpu.CompilerParams(dimension_semantics=("arbitrary", "arbitrary"),
                                             vmem_limit_bytes=VMEM_LIMIT),
        name="swa_attn",
    )(sinks, q, k, k, v, v, g)


def _layernorm(r, g, b):
    mu = jnp.mean(r, axis=-1, keepdims=True)
    d = r - mu
    var = jnp.mean(d * d, axis=-1, keepdims=True)
    return d * lax.rsqrt(var + 1e-5) * g + b


def _out_ffn_kernel(x_ref, ma_ref, mb_ref, mc_ref, md_ref, wo_ref, g1_ref, b1_ref,
                    wg_ref, wu_ref, wd_ref, g2_ref, b2_ref, o_ref, *, alpha):
    y = None
    for n, m_ref in enumerate((ma_ref, mb_ref, mc_ref, md_ref)):
        part = _dot(m_ref[...], wo_ref[GROUP_WIDTH * n:GROUP_WIDTH * (n + 1), :])
        y = part if y is None else y + part
    x1 = _layernorm(alpha * x_ref[...] + y, g1_ref[...], b1_ref[...])
    x1b = x1.astype(BF16)
    f = None
    for c in range(D_FF // F_CHUNK):
        cols = slice(F_CHUNK * c, F_CHUNK * (c + 1))
        gate = _dot(x1b, wg_ref[:, cols])
        up = _dot(x1b, wu_ref[:, cols])
        hid = (gate * (1.0 / (1.0 + jnp.exp(-gate))) * up).astype(BF16)
        part = _dot(hid, wd_ref[cols, :])
        f = part if f is None else f + part
    o_ref[...] = _layernorm(alpha * x1 + f, g2_ref[...], b2_ref[...])


def _out_ffn(x2, mixes, wo, g1, b1, wg, wu, wd, g2, b2, *, alpha):
    M = x2.shape[0]
    tm = TM_OUT
    row = lambda i: (i, 0)
    const = lambda i: (0, 0)
    resident = lambda shape: pl.BlockSpec(shape, const, pipeline_mode=pl.Buffered(1))
    vec = pl.BlockSpec((1, D_MODEL), const)
    in_specs = [pl.BlockSpec((tm, D_MODEL), row)]
    in_specs += [pl.BlockSpec((tm, GROUP_WIDTH), row)] * 4
    in_specs += [resident((D_MODEL, D_MODEL)), vec, vec,
                 resident((D_MODEL, D_FF)), resident((D_MODEL, D_FF)), resident((D_FF, D_MODEL)),
                 vec, vec]
    kern = functools.partial(_out_ffn_kernel, alpha=alpha)
    return pl.pallas_call(
        kern, out_shape=jax.ShapeDtypeStruct((M, D_MODEL), F32), grid=(M // tm,),
        in_specs=in_specs, out_specs=pl.BlockSpec((tm, D_MODEL), row),
        compiler_params=pltpu.CompilerParams(dimension_semantics=("arbitrary",),
                                             vmem_limit_bytes=VMEM_LIMIT),
        name="out_ffn",
    )(x2, *mixes, wo, g1, b1, wg, wu, wd, g2, b2)


def kernel(x, w_in, fox_b_f, mla_g_q, mla_g_kv, mla_w_uq, mla_w_ukv, swa_sinks, mix_g, w_o,
           ln1_g, ln1_b, w_gate, w_up, w_down, ln2_g, ln2_b):
    B, S, D = x.shape
    depth = w_in.shape[0]
    assert D == D_MODEL and S % TM_IN == 0 and S % TQ_SWA == 0 and (B * S) % TM_OUT == 0

    w_in_p = _gather_cols(w_in, _in_proj_columns()).astype(BF16)
    wq_p = _gather_cols(mla_w_uq, _mla_q_columns()).astype(BF16)
    wkv_p = _gather_cols(mla_w_ukv, _mla_kv_columns()).astype(BF16)
    order = _mix_row_order()
    wo_p = jnp.take(w_o, jnp.asarray(order, jnp.int32), axis=1).astype(BF16)
    mix_g_p = jnp.take(mix_g, jnp.asarray(order, jnp.int32), axis=1)
    wg_b, wu_b, wd_b = w_gate.astype(BF16), w_up.astype(BF16), w_down.astype(BF16)
    b_f = jnp.pad(fox_b_f.astype(F32), ((0, 0), (0, LANES - fox_b_f.shape[1])))
    tc, ts = _rope_slot_tables(S)
    idx_in = np.arange(TM_IN)
    tri_incl = jnp.asarray(idx_in[:, None] <= idx_in[None, :], BF16)
    idx_q = np.arange(TQ)
    tri_after = jnp.asarray(idx_q[:, None] > idx_q[None, :], BF16)

    k_cols_packed = tuple(LANES * (h // 2) for h in range(GROUP_HEADS))
    k_cols_slots = tuple(LANES * h for h in range(GROUP_HEADS))

    x2 = x.reshape(B * S, D)
    for l in range(depth):
        (qa, ka, va, cum, qb, kb, vb, qc, kc, vc, qd, kd, vd) = _in_proj(
            x2, w_in_p[l], wq_p[l], wkv_p[l], mla_g_q[l][None, :], mla_g_kv[l][None, :],
            b_f[l][None, :], tc, ts, tri_incl, B=B, S=S)
        g = mix_g_p[l][None, :]
        mix_a = _softmax_attn(qa, ka, va, cum, g[:, 0:256], B=B, S=S, k_cols=k_cols_packed, name="fox_attn")
        mix_b = _softmax_attn(qb, kb, vb, None, g[:, 256:512], B=B, S=S, k_cols=k_cols_slots, name="mla_attn")
        mix_c = _stick_attn(qc, kc, vc, tri_after, g[:, 512:768], B=B, S=S)
        mix_d = _swa_attn(swa_sinks[l].astype(F32), qd, kd, vd, g[:, 768:1024], B=B, S=S)
        x2 = _out_ffn(x2, (mix_a, mix_b, mix_c, mix_d), wo_p[l], ln1_g[l][None, :], ln1_b[l][None, :],
                      wg_b[l], wu_b[l], wd_b[l], ln2_g[l][None, :], ln2_b[l][None, :], alpha=ALPHA)
    return x2.reshape(B, S, D)
```

```python
import functools

import numpy as np
import jax
import jax.numpy as jnp
from jax import lax
from jax.experimental import pallas as pl
from jax.experimental.pallas import tpu as pltpu

F32 = jnp.float32
BF16 = jnp.bfloat16

D_MODEL = 1024
HEAD_DIM = 64
GROUP_HEADS = 4
GROUP_WIDTH = GROUP_HEADS * HEAD_DIM
MLA_Q_RANK = 256
MLA_KV_RANK = 128
MLA_ROPE = 32
SWA_WINDOW = 128
D_FF = 2816
NEG_INF = -1e30
DEPTH = 4
ALPHA = (2.0 * DEPTH) ** 0.25
ROPE_THETA = 10000.0

LANES = 128
VMEM_LIMIT = 52 * 1024 * 1024

C_FQ, C_FK, C_FV = 0, 256, 512
C_CQ, C_CKV = 768, 1024
C_SQ, C_SK, C_SV = 1152, 1408, 1664
C_WQ, C_WK, C_WV = 1920, 2176, 2304
C_KR1, C_KR2, C_GATE = 2432, 2560, 2688
IN_PERM_WIDTH = 2816

TM_IN = 512
TM_OUT = 512
TQ = 512
TRI_W = 256
TQ_SWA = 512
F_CHUNK = 256


def _nt_dot(a, b):
    return lax.dot_general(a, b, (((1,), (1,)), ((), ())), preferred_element_type=F32)


def _dot(a, b):
    return jnp.dot(a, b, preferred_element_type=F32)


def _log_sigmoid(x):
    return jnp.minimum(x, 0.0) - jnp.log(1.0 + jnp.exp(-jnp.abs(x)))


def _in_proj_columns():
    sizes = (256, 256, 256, 4, MLA_Q_RANK, MLA_KV_RANK, MLA_ROPE, 256, 256, 256, 256, 128, 128)
    starts = np.concatenate([[0], np.cumsum(sizes)[:-1]])
    (fq, fk, fv, fg, cq, ckv, kr, sq, sk, sv, wq, wk, wv) = [int(s) for s in starts]
    idx = np.full((IN_PERM_WIDTH,), -1, np.int64)

    def put(dst, src, n):
        idx[dst:dst + n] = np.arange(src, src + n)

    put(C_FQ, fq, 256); put(C_FK, fk, 256); put(C_FV, fv, 256)
    put(C_CQ, cq, MLA_Q_RANK); put(C_CKV, ckv, MLA_KV_RANK)
    put(C_SQ, sq, 256); put(C_SK, sk, 256); put(C_SV, sv, 256)
    for slot, head in enumerate((0, 2, 1, 3)):
        put(C_WQ + 64 * slot, wq + 64 * head, 64)
    put(C_WK, wk, 128); put(C_WV, wv, 128)
    half = MLA_ROPE // 2
    put(C_KR1 + 64, kr, MLA_ROPE)
    put(C_KR2 + 64, kr + half, half)
    put(C_KR2 + 64 + half, kr, half)
    put(C_GATE, fg, 4)
    return idx


def _gather_cols(w, idx):
    safe = np.where(idx < 0, 0, idx)
    out = jnp.take(w, jnp.asarray(safe, jnp.int32), axis=-1)
    return jnp.where(jnp.asarray(idx >= 0), out, 0.0)


def _mla_q_columns():
    half = MLA_ROPE // 2
    per = HEAD_DIM + MLA_ROPE
    idx = np.full((2 * GROUP_HEADS * LANES,), -1, np.int64)
    for h in range(GROUP_HEADS):
        idx[LANES * h:LANES * h + per] = np.arange(per * h, per * h + per)
        base = GROUP_HEADS * LANES + LANES * h + HEAD_DIM
        idx[base:base + half] = np.arange(per * h + HEAD_DIM + half, per * h + per)
        idx[base + half:base + 2 * half] = np.arange(per * h + HEAD_DIM, per * h + HEAD_DIM + half)
    return idx


def _mla_kv_columns():
    idx = np.full((GROUP_HEADS * LANES + GROUP_WIDTH,), -1, np.int64)
    for h in range(GROUP_HEADS):
        idx[LANES * h:LANES * h + HEAD_DIM] = np.arange(2 * HEAD_DIM * h, 2 * HEAD_DIM * h + HEAD_DIM)
        v0 = GROUP_HEADS * LANES + HEAD_DIM * h
        idx[v0:v0 + HEAD_DIM] = np.arange(2 * HEAD_DIM * h + HEAD_DIM, 2 * HEAD_DIM * (h + 1))
    return idx


def _mix_row_order():
    idx = np.arange(4 * GROUP_WIDTH)
    base = 3 * GROUP_WIDTH
    for slot, head in enumerate((0, 2, 1, 3)):
        idx[base + 64 * slot:base + 64 * (slot + 1)] = np.arange(base + 64 * head, base + 64 * (head + 1))
    return idx


def _rope_slot_tables(S):
    pos = jnp.arange(S, dtype=F32)
    inv = ROPE_THETA ** (-jnp.arange(0, MLA_ROPE, 2, dtype=F32) / MLA_ROPE)
    ang = pos[:, None] * inv[None, :]
    cos, sin = jnp.cos(ang), jnp.sin(ang)
    ones = jnp.ones((S, HEAD_DIM), F32)
    zeros32 = jnp.zeros((S, LANES - HEAD_DIM - MLA_ROPE), F32)
    tc = jnp.concatenate([ones, cos, cos, zeros32], axis=1)
    ts = jnp.concatenate([jnp.zeros((S, HEAD_DIM), F32), -sin, sin, zeros32], axis=1)
    return tc, ts


def _in_proj_kernel(x_ref, w_ref, wq_ref, wkv_ref, gq_ref, gkv_ref, bf_ref, tc_ref, ts_ref, tri_ref,
                    qa_ref, ka_ref, va_ref, cum_ref, qb_ref, kb_ref, vb_ref,
                    qc_ref, kc_ref, vc_ref, qd_ref, kd_ref, vd_ref, carry_ref,
                    *, tiles_per_seq, mla_scale):
    tm = x_ref.shape[0]
    i = pl.program_id(0)
    xb = x_ref[...].astype(BF16)

    def proj(c0, n):
        return _dot(xb, w_ref[:, c0:c0 + n])

    lo_half = lax.broadcasted_iota(jnp.int32, (tm, LANES), 1) < HEAD_DIM

    def store_slots(ref, packed, scale):
        for p in range(2):
            pair = packed[:, LANES * p:LANES * (p + 1)] * scale
            ref[:, 2 * LANES * p:2 * LANES * p + LANES] = jnp.where(lo_half, pair, 0.0).astype(BF16)
            ref[:, 2 * LANES * p + LANES:2 * LANES * (p + 1)] = jnp.where(lo_half, 0.0, pair).astype(BF16)

    head_scale = HEAD_DIM ** -0.5
    store_slots(qa_ref, proj(C_FQ, 256), head_scale)
    ka_ref[...] = proj(C_FK, 256).astype(BF16)
    va_ref[...] = proj(C_FV, 256).astype(BF16)
    store_slots(qc_ref, proj(C_SQ, 256), head_scale)
    kc_ref[...] = proj(C_SK, 256).astype(BF16)
    vc_ref[...] = proj(C_SV, 256).astype(BF16)
    store_slots(qd_ref, proj(C_WQ, 256), head_scale)
    kd_ref[...] = proj(C_WK, 128).astype(BF16)
    vd_ref[...] = proj(C_WV, 128).astype(BF16)

    tc = tc_ref[...]
    ts = ts_ref[...]
    cq = proj(C_CQ, MLA_Q_RANK)
    cqn = (cq * lax.rsqrt(jnp.mean(cq * cq, axis=-1, keepdims=True) + 1e-6) * gq_ref[...]).astype(BF16)
    q12 = _dot(cqn, wq_ref[...])
    for h in range(GROUP_HEADS):
        q1 = q12[:, LANES * h:LANES * (h + 1)]
        q2 = q12[:, LANES * (GROUP_HEADS + h):LANES * (GROUP_HEADS + h + 1)]
        qb_ref[:, LANES * h:LANES * (h + 1)] = ((q1 * tc + q2 * ts) * mla_scale).astype(BF16)
    ckv = proj(C_CKV, MLA_KV_RANK)
    ckvn = (ckv * lax.rsqrt(jnp.mean(ckv * ckv, axis=-1, keepdims=True) + 1e-6) * gkv_ref[...]).astype(BF16)
    kv = _dot(ckvn, wkv_ref[...])
    k_rope = proj(C_KR1, LANES) * tc + proj(C_KR2, LANES) * ts
    for h in range(GROUP_HEADS):
        kb_ref[:, LANES * h:LANES * (h + 1)] = (kv[:, LANES * h:LANES * (h + 1)] + k_rope).astype(BF16)
    vb_ref[...] = kv[:, GROUP_HEADS * LANES:].astype(BF16)

    log_f = _log_sigmoid(proj(C_GATE, LANES) + bf_ref[...])
    lf = jnp.transpose(log_f)[0:8, :]
    hi = lf.astype(BF16)
    lo = (lf - hi.astype(F32)).astype(BF16)
    tri = tri_ref[...]
    local = _dot(hi, tri) + _dot(lo, tri)
    carry = jnp.where(i % tiles_per_seq == 0, 0.0, carry_ref[:, 0:1])
    cum = local + carry
    carry_ref[...] = jnp.broadcast_to(cum[:, tm - 1:tm], carry_ref.shape)
    tk = cum_ref.shape[-1]
    for r in range(tm // tk):
        cum_ref[0, r] = cum[:, tk * r:tk * (r + 1)]


def _in_proj(x2, w_in_p, wq_p, wkv_p, g_q, g_kv, b_f, tc, ts, tri, *, B, S):
    M = x2.shape[0]
    tm = TM_IN
    tiles_per_seq = S // tm
    row = lambda i: (i, 0)
    const = lambda i: (0, 0)
    pos = lambda i: (i % tiles_per_seq, 0)
    bf = lambda w: jax.ShapeDtypeStruct((M, w), BF16)
    out_shape = (bf(512), bf(256), bf(256),
                 jax.ShapeDtypeStruct((B, S // TQ, 8, TQ), F32),
                 bf(512), bf(512), bf(256),
                 bf(512), bf(256), bf(256),
                 bf(512), bf(128), bf(128))
    ob = lambda w: pl.BlockSpec((tm, w), row)
    out_specs = (ob(512), ob(256), ob(256),
                 pl.BlockSpec((1, tm // TQ, 8, TQ), lambda i: (i // tiles_per_seq, i % tiles_per_seq, 0, 0)),
                 ob(512), ob(512), ob(256),
                 ob(512), ob(256), ob(256),
                 ob(512), ob(128), ob(128))
    in_specs = [
        pl.BlockSpec((tm, D_MODEL), row),
        pl.BlockSpec((D_MODEL, IN_PERM_WIDTH), const),
        pl.BlockSpec(wq_p.shape, const),
        pl.BlockSpec(wkv_p.shape, const),
        pl.BlockSpec((1, MLA_Q_RANK), const),
        pl.BlockSpec((1, MLA_KV_RANK), const),
        pl.BlockSpec((1, LANES), const),
        pl.BlockSpec((tm, LANES), pos),
        pl.BlockSpec((tm, LANES), pos),
        pl.BlockSpec((tm, tm), const),
    ]
    kern = functools.partial(_in_proj_kernel, tiles_per_seq=tiles_per_seq,
                             mla_scale=float((HEAD_DIM + MLA_ROPE) ** -0.5))
    return pl.pallas_call(
        kern, out_shape=out_shape, grid=(M // tm,), in_specs=in_specs, out_specs=out_specs,
        scratch_shapes=[pltpu.VMEM((8, LANES), F32)],
        compiler_params=pltpu.CompilerParams(dimension_semantics=("arbitrary",),
                                             vmem_limit_bytes=VMEM_LIMIT),
        name="in_proj",
    )(x2, w_in_p, wq_p, wkv_p, g_q, g_kv, b_f, tc, ts, tri)


def _pair_norm_store(o_ref, heads, g_ref, rows=None):
    t = heads[0].shape[0]
    lo_half = lax.broadcasted_iota(jnp.int32, (t, LANES), 1) < HEAD_DIM
    pairs = [jnp.where(lo_half, heads[2 * p], heads[2 * p + 1]) for p in range(2)]
    ss = sum(jnp.sum(p * p, axis=-1, keepdims=True) for p in pairs)
    inv = lax.rsqrt(ss * (1.0 / GROUP_WIDTH) + 1e-6)
    for p in range(2):
        val = (pairs[p] * inv * g_ref[:, LANES * p:LANES * (p + 1)]).astype(o_ref.dtype)
        if rows is None:
            o_ref[:, LANES * p:LANES * (p + 1)] = val
        else:
            o_ref[rows, LANES * p:LANES * (p + 1)] = val


def _softmax_attn_kernel(*refs, k_cols, has_bias):
    if has_bias:
        q_ref, k_ref, v_ref, cum_ref, g_ref, o_ref, m_ref, l_ref, acc_ref = refs
    else:
        q_ref, k_ref, v_ref, g_ref, o_ref, m_ref, l_ref, acc_ref = refs
        cum_ref = None
    tq = q_ref.shape[0]
    tk = tq
    i = pl.program_id(1)
    m_ref[...] = jnp.full(m_ref.shape, NEG_INF, F32)
    l_ref[...] = jnp.zeros(l_ref.shape, F32)
    acc_ref[...] = jnp.zeros(acc_ref.shape, F32)

    def step(kb, masked):
        r0 = pl.multiple_of(kb * tk, tk)
        for h in range(GROUP_HEADS):
            kc = k_cols[h]
            vc = LANES * (h // 2)
            s = _nt_dot(q_ref[:, LANES * h:LANES * (h + 1)], k_ref[pl.ds(r0, tk), kc:kc + LANES])
            if has_bias:
                s = s - cum_ref[0, kb][h:h + 1, :]
            if masked:
                causal = (lax.broadcasted_iota(jnp.int32, (tq, tk), 1)
                          <= lax.broadcasted_iota(jnp.int32, (tq, tk), 0))
                s = jnp.where(causal, s, NEG_INF)
            m_prev = m_ref[h]
            m_next = jnp.maximum(m_prev, jnp.max(s, axis=-1, keepdims=True))
            alpha = jnp.exp(m_prev - m_next)
            p = jnp.exp(s - jnp.concatenate([m_next] * (tk // LANES), axis=1))
            l_ref[h] = alpha * l_ref[h] + jnp.sum(p, axis=-1, keepdims=True)
            m_ref[h] = m_next
            acc_ref[h] = alpha * acc_ref[h] + _dot(p.astype(BF16), v_ref[pl.ds(r0, tk), vc:vc + LANES])

    def body(kb, carry):
        step(kb, False)
        return carry

    lax.fori_loop(0, i, body, 0)
    step(i, True)
    heads = [acc_ref[h] / l_ref[h] for h in range(GROUP_HEADS)]
    _pair_norm_store(o_ref, heads, g_ref)


def _softmax_attn(q, k, v, cum, g, *, B, S, k_cols, name):
    M = q.shape[0]
    nq = S // TQ
    kw = k.shape[1]
    has_bias = cum is not None
    in_specs = [
        pl.BlockSpec((TQ, 4 * LANES), lambda b, i: (b * nq + i, 0)),
        pl.BlockSpec((S, kw), lambda b, i: (b, 0)),
        pl.BlockSpec((S, GROUP_WIDTH), lambda b, i: (b, 0)),
    ]
    args = [q, k, v]
    if has_bias:
        in_specs.append(pl.BlockSpec((1, nq, 8, TQ), lambda b, i: (b, 0, 0, 0)))
        args.append(cum)
    in_specs.append(pl.BlockSpec((1, GROUP_WIDTH), lambda b, i: (0, 0)))
    args.append(g)
    kern = functools.partial(_softmax_attn_kernel, k_cols=k_cols, has_bias=has_bias)
    return pl.pallas_call(
        kern, out_shape=jax.ShapeDtypeStruct((M, GROUP_WIDTH), BF16), grid=(B, nq),
        in_specs=in_specs,
        out_specs=pl.BlockSpec((TQ, GROUP_WIDTH), lambda b, i: (b * nq + i, 0)),
        scratch_shapes=[pltpu.VMEM((GROUP_HEADS, TQ, LANES), F32)] * 3,
        compiler_params=pltpu.CompilerParams(dimension_semantics=("arbitrary", "arbitrary"),
                                             vmem_limit_bytes=VMEM_LIMIT),
        name=name,
    )(*args)


def _stick_kernel(q_ref, k_ref, v_ref, tri_ref, g_ref, o_ref, rest_ref, acc_ref):
    tq = q_ref.shape[0]
    tk = tq
    tw = tri_ref.shape[0]
    i = pl.program_id(1)
    rest_ref[...] = jnp.zeros(rest_ref.shape, F32)
    acc_ref[...] = jnp.zeros(acc_ref.shape, F32)

    def block(kb, masked):
        r0 = pl.multiple_of(kb * tk, tk)
        tri = tri_ref[...]
        for h in range(GROUP_HEADS):
            c0 = LANES * (h // 2)
            z = _nt_dot(q_ref[:, LANES * h:LANES * (h + 1)], k_ref[pl.ds(r0, tk), c0:c0 + LANES])
            log_1mb = -(jnp.maximum(z, 0.0) + jnp.log(1.0 + jnp.exp(-jnp.abs(z))))
            if masked:
                strict = (lax.broadcasted_iota(jnp.int32, (tq, tk), 1)
                          < lax.broadcasted_iota(jnp.int32, (tq, tk), 0))
                log_1mb = jnp.where(strict, log_1mb, 0.0)
            rest = rest_ref[h]
            parts = [None] * (tk // tw)
            for c in reversed(range(tk // tw)):
                sub = log_1mb[:, tw * c:tw * (c + 1)]
                hi = sub.astype(BF16)
                lo = (sub - hi.astype(F32)).astype(BF16)
                parts[c] = (_dot(hi, tri) + _dot(lo, tri)
                            + jnp.concatenate([rest] * (tw // LANES), axis=1))
                rest = rest + jnp.sum(sub, axis=-1, keepdims=True)
            rest_ref[h] = rest
            a = jnp.exp(z + log_1mb + jnp.concatenate(parts, axis=1))
            if masked:
                a = jnp.where(strict, a, 0.0)
            acc_ref[h] = acc_ref[h] + _dot(a.astype(BF16), v_ref[pl.ds(r0, tk), c0:c0 + LANES])

    block(i, True)

    def body(t, carry):
        block(i - 1 - t, False)
        return carry

    lax.fori_loop(0, i, body, 0)
    _pair_norm_store(o_ref, [acc_ref[h] for h in range(GROUP_HEADS)], g_ref)


def _stick_attn(q, k, v, tri, g, *, B, S):
    M = q.shape[0]
    nq = S // TQ
    return pl.pallas_call(
        _stick_kernel, out_shape=jax.ShapeDtypeStruct((M, GROUP_WIDTH), BF16), grid=(B, nq),
        in_specs=[
            pl.BlockSpec((TQ, 4 * LANES), lambda b, i: (b * nq + i, 0)),
            pl.BlockSpec((S, GROUP_WIDTH), lambda b, i: (b, 0)),
            pl.BlockSpec((S, GROUP_WIDTH), lambda b, i: (b, 0)),
            pl.BlockSpec((TRI_W, TRI_W), lambda b, i: (0, 0)),
            pl.BlockSpec((1, GROUP_WIDTH), lambda b, i: (0, 0)),
        ],
        out_specs=pl.BlockSpec((TQ, GROUP_WIDTH), lambda b, i: (b * nq + i, 0)),
        scratch_shapes=[pltpu.VMEM((GROUP_HEADS, TQ, LANES), F32)] * 2,
        compiler_params=pltpu.CompilerParams(dimension_semantics=("arbitrary", "arbitrary"),
                                             vmem_limit_bytes=VMEM_LIMIT),
        name="stick_attn",
    )(q, k, v, tri, g)


def _swa_kernel(sink_ref, q_ref, kc_ref, kp_ref, vc_ref, vp_ref, g_ref, o_ref, *, slopes):
    tq = q_ref.shape[0]
    w = SWA_WINDOW
    i = pl.program_id(1)
    row = lax.broadcasted_iota(jnp.int32, (w, w), 0)
    col = lax.broadcasted_iota(jnp.int32, (w, w), 1)
    in_cur = col <= row
    in_prev = col > row
    dist_cur = (row - col).astype(F32)
    dist_prev = (row + w - col).astype(F32)
    for r in range(tq // w):
        rows = slice(w * r, w * (r + 1))
        k_cur = kc_ref[rows, :]
        v_cur = vc_ref[rows, :]
        if r == 0:
            k_prev, v_prev = kp_ref[...], vp_ref[...]
        else:
            k_prev, v_prev = kc_ref[w * (r - 1):w * r, :], vc_ref[w * (r - 1):w * r, :]
        heads = []
        for slot in range(GROUP_HEADS):
            head = slot // 2 + 2 * (slot % 2)
            qh = q_ref[rows, LANES * slot:LANES * (slot + 1)]
            sink = sink_ref[head]
            s_cur = jnp.where(in_cur, _nt_dot(qh, k_cur) - slopes[head] * dist_cur, NEG_INF)
            s_prev = jnp.where(in_prev, _nt_dot(qh, k_prev) - slopes[head] * dist_prev, NEG_INF)
            if r == 0:
                s_prev = jnp.where(i > 0, s_prev, NEG_INF)
            m = jnp.maximum(jnp.maximum(jnp.max(s_cur, axis=-1, keepdims=True),
                                        jnp.max(s_prev, axis=-1, keepdims=True)), sink)
            p_cur = jnp.exp(s_cur - m)
            p_prev = jnp.exp(s_prev - m)
            den = (jnp.sum(p_cur, axis=-1, keepdims=True) + jnp.sum(p_prev, axis=-1, keepdims=True)
                   + jnp.exp(sink - m))
            pv = _dot(p_cur.astype(BF16), v_cur) + _dot(p_prev.astype(BF16), v_prev)
            heads.append(pv / den)
        _pair_norm_store(o_ref, heads, g_ref, rows=rows)


def _swa_attn(sinks, q, k, v, g, *, B, S):
    M = q.shape[0]
    tq = TQ_SWA
    nq = S // tq
    per = tq // SWA_WINDOW
    slopes = tuple(float(2.0 ** (-8.0 * (h + 1) / GROUP_HEADS)) for h in range(GROUP_HEADS))
    cur = lambda b, i: (b * nq + i, 0)
    prev = lambda b, i: (b * nq * per + jnp.maximum(i * per - 1, 0), 0)
    kern = functools.partial(_swa_kernel, slopes=slopes)
    return pl.pallas_call(
        kern, out_shape=jax.ShapeDtypeStruct((M, GROUP_WIDTH), BF16), grid=(B, nq),
        in_specs=[
            pl.BlockSpec(memory_space=pltpu.SMEM),
            pl.BlockSpec((tq, 4 * LANES), cur),
            pl.BlockSpec((tq, LANES), cur),
            pl.BlockSpec((SWA_WINDOW, LANES), prev),
            pl.BlockSpec((tq, LANES), cur),
            pl.BlockSpec((SWA_WINDOW, LANES), prev),
            pl.BlockSpec((1, GROUP_WIDTH), lambda b, i: (0, 0)),
        ],
        out_specs=pl.BlockSpec((tq, GROUP_WIDTH), cur),
        compiler_params=pltpu.CompilerParams(dimension_semantics=("arbitrary", "arbitrary"),
                                             vmem_limit_bytes=VMEM_LIMIT),
        name="swa_attn",
    )(sinks, q, k, k, v, v, g)


def _layernorm(r, g, b):
    mu = jnp.mean(r, axis=-1, keepdims=True)
    d = r - mu
    var = jnp.mean(d * d, axis=-1, keepdims=True)
    return d * lax.rsqrt(var + 1e-5) * g + b


def _out_ffn_kernel(x_ref, ma_ref, mb_ref, mc_ref, md_ref, wo_ref, g1_ref, b1_ref,
                    wg_ref, wu_ref, wd_ref, g2_ref, b2_ref, o_ref, *, alpha):
    y = None
    for n, m_ref in enumerate((ma_ref, mb_ref, mc_ref, md_ref)):
        part = _dot(m_ref[...], wo_ref[GROUP_WIDTH * n:GROUP_WIDTH * (n + 1), :])
        y = part if y is None else y + part
    x1 = _layernorm(alpha * x_ref[...] + y, g1_ref[...], b1_ref[...])
    x1b = x1.astype(BF16)
    f = None
    for c in range(D_FF // F_CHUNK):
        cols = slice(F_CHUNK * c, F_CHUNK * (c + 1))
        gate = _dot(x1b, wg_ref[:, cols])
        up = _dot(x1b, wu_ref[:, cols])
        hid = (gate * (1.0 / (1.0 + jnp.exp(-gate))) * up).astype(BF16)
        part = _dot(hid, wd_ref[cols, :])
        f = part if f is None else f + part
    o_ref[...] = _layernorm(alpha * x1 + f, g2_ref[...], b2_ref[...])


def _out_ffn(x2, mixes, wo, g1, b1, wg, wu, wd, g2, b2, *, alpha):
    M = x2.shape[0]
    tm = TM_OUT
    row = lambda i: (i, 0)
    const = lambda i: (0, 0)
    resident = lambda shape: pl.BlockSpec(shape, const, pipeline_mode=pl.Buffered(1))
    vec = pl.BlockSpec((1, D_MODEL), const)
    in_specs = [pl.BlockSpec((tm, D_MODEL), row)]
    in_specs += [pl.BlockSpec((tm, GROUP_WIDTH), row)] * 4
    in_specs += [resident((D_MODEL, D_MODEL)), vec, vec,
                 resident((D_MODEL, D_FF)), resident((D_MODEL, D_FF)), resident((D_FF, D_MODEL)),
                 vec, vec]
    kern = functools.partial(_out_ffn_kernel, alpha=alpha)
    return pl.pallas_call(
        kern, out_shape=jax.ShapeDtypeStruct((M, D_MODEL), F32), grid=(M // tm,),
        in_specs=in_specs, out_specs=pl.BlockSpec((tm, D_MODEL), row),
        compiler_params=pltpu.CompilerParams(dimension_semantics=("arbitrary",),
                                             vmem_limit_bytes=VMEM_LIMIT),
        name="out_ffn",
    )(x2, *mixes, wo, g1, b1, wg, wu, wd, g2, b2)


def kernel(x, w_in, fox_b_f, mla_g_q, mla_g_kv, mla_w_uq, mla_w_ukv, swa_sinks, mix_g, w_o,
           ln1_g, ln1_b, w_gate, w_up, w_down, ln2_g, ln2_b):
    B, S, D = x.shape
    depth = w_in.shape[0]
    assert D == D_MODEL and S % TM_IN == 0 and S % TQ_SWA == 0 and (B * S) % TM_OUT == 0

    w_in_p = _gather_cols(w_in, _in_proj_columns()).astype(BF16)
    wq_p = _gather_cols(mla_w_uq, _mla_q_columns()).astype(BF16)
    wkv_p = _gather_cols(mla_w_ukv, _mla_kv_columns()).astype(BF16)
    order = _mix_row_order()
    wo_p = jnp.take(w_o, jnp.asarray(order, jnp.int32), axis=1).astype(BF16)
    mix_g_p = jnp.take(mix_g, jnp.asarray(order, jnp.int32), axis=1)
    wg_b, wu_b, wd_b = w_gate.astype(BF16), w_up.astype(BF16), w_down.astype(BF16)
    b_f = jnp.pad(fox_b_f.astype(F32), ((0, 0), (0, LANES - fox_b_f.shape[1])))
    tc, ts = _rope_slot_tables(S)
    idx_in = np.arange(TM_IN)
    tri_incl = jnp.asarray(idx_in[:, None] <= idx_in[None, :], BF16)
    idx_q = np.arange(TRI_W)
    tri_after = jnp.asarray(idx_q[:, None] > idx_q[None, :], BF16)

    k_cols_packed = tuple(LANES * (h // 2) for h in range(GROUP_HEADS))
    k_cols_slots = tuple(LANES * h for h in range(GROUP_HEADS))

    x2 = x.reshape(B * S, D)
    for l in range(depth):
        (qa, ka, va, cum, qb, kb, vb, qc, kc, vc, qd, kd, vd) = _in_proj(
            x2, w_in_p[l], wq_p[l], wkv_p[l], mla_g_q[l][None, :], mla_g_kv[l][None, :],
            b_f[l][None, :], tc, ts, tri_incl, B=B, S=S)
        g = mix_g_p[l][None, :]
        mix_a = _softmax_attn(qa, ka, va, cum, g[:, 0:256], B=B, S=S, k_cols=k_cols_packed, name="fox_attn")
        mix_b = _softmax_attn(qb, kb, vb, None, g[:, 256:512], B=B, S=S, k_cols=k_cols_slots, name="mla_attn")
        mix_c = _stick_attn(qc, kc, vc, tri_after, g[:, 512:768], B=B, S=S)
        mix_d = _swa_attn(swa_sinks[l].astype(F32), qd, kd, vd, g[:, 768:1024], B=B, S=S)
        x2 = _out_ffn(x2, (mix_a, mix_b, mix_c, mix_d), wo_p[l], ln1_g[l][None, :], ln1_b[l][None, :],
                      wg_b[l], wu_b[l], wd_b[l], ln2_g[l][None, :], ln2_b[l][None, :], alpha=ALPHA)
    return x2.reshape(B, S, D)
```

```python
import functools

import numpy as np
import jax
import jax.numpy as jnp
from jax import lax
from jax.experimental import pallas as pl
from jax.experimental.pallas import tpu as pltpu

F32 = jnp.float32
BF16 = jnp.bfloat16

D_MODEL = 1024
HEAD_DIM = 64
GROUP_HEADS = 4
GROUP_WIDTH = GROUP_HEADS * HEAD_DIM
MLA_Q_RANK = 256
MLA_KV_RANK = 128
MLA_ROPE = 32
SWA_WINDOW = 128
D_FF = 2816
NEG_INF = -1e30
DEPTH = 4
ALPHA = (2.0 * DEPTH) ** 0.25
ROPE_THETA = 10000.0

LANES = 128
VMEM_LIMIT = 52 * 1024 * 1024

C_FQ, C_FK, C_FV = 0, 256, 512
C_CQ = 768
C_SQ, C_SK, C_SV = 1024, 1280, 1536
C_WQ, C_WK, C_WV = 1792, 2048, 2176
C_KR1, C_KR2 = 2304, 2432
C_CKV, C_GATE = 2560, 2688
IN_PERM_WIDTH = 2816
LOG2E = 1.4426950408889634

TM_IN = 512
TM_OUT = 512
TQ = 512
TRI_W = 256
TQ_SWA = 512
F_CHUNK = 256


def _nt_dot(a, b):
    return lax.dot_general(a, b, (((1,), (1,)), ((), ())), preferred_element_type=F32)


def _dot(a, b):
    return jnp.dot(a, b, preferred_element_type=F32)


def _log_sigmoid(x):
    return jnp.minimum(x, 0.0) - jnp.log(1.0 + jnp.exp(-jnp.abs(x)))


def _in_proj_columns():
    sizes = (256, 256, 256, 4, MLA_Q_RANK, MLA_KV_RANK, MLA_ROPE, 256, 256, 256, 256, 128, 128)
    starts = np.concatenate([[0], np.cumsum(sizes)[:-1]])
    (fq, fk, fv, fg, cq, ckv, kr, sq, sk, sv, wq, wk, wv) = [int(s) for s in starts]
    idx = np.full((IN_PERM_WIDTH,), -1, np.int64)

    def put(dst, src, n):
        idx[dst:dst + n] = np.arange(src, src + n)

    put(C_FQ, fq, 256); put(C_FK, fk, 256); put(C_FV, fv, 256)
    put(C_CQ, cq, MLA_Q_RANK); put(C_CKV, ckv, MLA_KV_RANK)
    put(C_SQ, sq, 256); put(C_SK, sk, 256); put(C_SV, sv, 256)
    for slot, head in enumerate((0, 2, 1, 3)):
        put(C_WQ + 64 * slot, wq + 64 * head, 64)
    put(C_WK, wk, 128); put(C_WV, wv, 128)
    half = MLA_ROPE // 2
    put(C_KR1 + 64, kr, MLA_ROPE)
    put(C_KR2 + 64, kr + half, half)
    put(C_KR2 + 64 + half, kr, half)
    put(C_GATE, fg, 4)
    return idx


def _gather_cols(w, idx):
    pieces, start = [], 0
    n = len(idx)
    while start < n:
        stop = start + 1
        if idx[start] < 0:
            while stop < n and idx[stop] < 0:
                stop += 1
            pieces.append(jnp.zeros(w.shape[:-1] + (stop - start,), w.dtype))
        else:
            while stop < n and idx[stop] == idx[stop - 1] + 1:
                stop += 1
            pieces.append(w[..., int(idx[start]):int(idx[stop - 1]) + 1])
        start = stop
    return jnp.concatenate(pieces, axis=-1)


def _mla_q_columns():
    half = MLA_ROPE // 2
    per = HEAD_DIM + MLA_ROPE
    idx = np.full((2 * GROUP_HEADS * LANES,), -1, np.int64)
    for h in range(GROUP_HEADS):
        idx[LANES * h:LANES * h + per] = np.arange(per * h, per * h + per)
        base = GROUP_HEADS * LANES + LANES * h + HEAD_DIM
        idx[base:base + half] = np.arange(per * h + HEAD_DIM + half, per * h + per)
        idx[base + half:base + 2 * half] = np.arange(per * h + HEAD_DIM, per * h + HEAD_DIM + half)
    return idx


def _mla_kv_columns():
    idx = np.full((GROUP_HEADS * LANES + GROUP_WIDTH,), -1, np.int64)
    for h in range(GROUP_HEADS):
        idx[LANES * h:LANES * h + HEAD_DIM] = np.arange(2 * HEAD_DIM * h, 2 * HEAD_DIM * h + HEAD_DIM)
        v0 = GROUP_HEADS * LANES + HEAD_DIM * h
        idx[v0:v0 + HEAD_DIM] = np.arange(2 * HEAD_DIM * h + HEAD_DIM, 2 * HEAD_DIM * (h + 1))
    return idx


def _mix_row_order():
    idx = np.arange(4 * GROUP_WIDTH)
    base = 3 * GROUP_WIDTH
    for slot, head in enumerate((0, 2, 1, 3)):
        idx[base + 64 * slot:base + 64 * (slot + 1)] = np.arange(base + 64 * head, base + 64 * (head + 1))
    return idx


def _rope_slot_tables(S):
    pos = jnp.arange(S, dtype=F32)
    inv = ROPE_THETA ** (-jnp.arange(0, MLA_ROPE, 2, dtype=F32) / MLA_ROPE)
    ang = pos[:, None] * inv[None, :]
    cos, sin = jnp.cos(ang), jnp.sin(ang)
    ones = jnp.ones((S, HEAD_DIM), F32)
    zeros32 = jnp.zeros((S, LANES - HEAD_DIM - MLA_ROPE), F32)
    tc = jnp.concatenate([ones, cos, cos, zeros32], axis=1)
    ts = jnp.concatenate([jnp.zeros((S, HEAD_DIM), F32), -sin, sin, zeros32], axis=1)
    return tc, ts


def _in_proj_kernel(x_ref, w_ref, wq_ref, wkv_ref, gq_ref, gkv_ref, bf_ref, tc_ref, ts_ref, tri_ref,
                    qa_ref, ka_ref, va_ref, cum_ref, qb_ref, kb_ref, vb_ref,
                    qc_ref, kc_ref, vc_ref, qd_ref, kd_ref, vd_ref, carry_ref,
                    *, tiles_per_seq, mla_scale):
    tm = x_ref.shape[0]
    i = pl.program_id(0)
    xb = x_ref[...].astype(BF16)

    def proj(c0, n):
        return _dot(xb, w_ref[:, c0:c0 + n])

    lo_half = lax.broadcasted_iota(jnp.int32, (tm, LANES), 1) < HEAD_DIM

    def store_slots(ref, packed, scale):
        for p in range(2):
            pair = packed[:, LANES * p:LANES * (p + 1)] * scale
            ref[:, 2 * LANES * p:2 * LANES * p + LANES] = jnp.where(lo_half, pair, 0.0).astype(BF16)
            ref[:, 2 * LANES * p + LANES:2 * LANES * (p + 1)] = jnp.where(lo_half, 0.0, pair).astype(BF16)

    head_scale = HEAD_DIM ** -0.5 * LOG2E
    store_slots(qa_ref, proj(C_FQ, 256), head_scale)
    ka_ref[...] = proj(C_FK, 256).astype(BF16)
    va_ref[...] = proj(C_FV, 256).astype(BF16)
    store_slots(qc_ref, proj(C_SQ, 256), head_scale)
    kc_ref[...] = proj(C_SK, 256).astype(BF16)
    vc_ref[...] = proj(C_SV, 256).astype(BF16)
    store_slots(qd_ref, proj(C_WQ, 256), head_scale)
    kv_d = proj(C_WK, 2 * LANES)
    kd_ref[...] = kv_d[:, :LANES].astype(BF16)
    vd_ref[...] = kv_d[:, LANES:].astype(BF16)

    tc = tc_ref[...]
    ts = ts_ref[...]
    cq = proj(C_CQ, MLA_Q_RANK)
    cqn = (cq * lax.rsqrt(jnp.mean(cq * cq, axis=-1, keepdims=True) + 1e-6) * gq_ref[...]).astype(BF16)
    q12 = _dot(cqn, wq_ref[...])
    for h in range(GROUP_HEADS):
        q1 = q12[:, LANES * h:LANES * (h + 1)]
        q2 = q12[:, LANES * (GROUP_HEADS + h):LANES * (GROUP_HEADS + h + 1)]
        qb_ref[:, LANES * h:LANES * (h + 1)] = ((q1 * tc + q2 * ts) * mla_scale).astype(BF16)
    ckv_gate = proj(C_CKV, 2 * LANES)
    ckv = ckv_gate[:, :MLA_KV_RANK]
    ckvn = (ckv * lax.rsqrt(jnp.mean(ckv * ckv, axis=-1, keepdims=True) + 1e-6) * gkv_ref[...]).astype(BF16)
    kv = _dot(ckvn, wkv_ref[...])
    kr12 = proj(C_KR1, 2 * LANES)
    k_rope = kr12[:, :LANES] * tc + kr12[:, LANES:] * ts
    for h in range(GROUP_HEADS):
        kb_ref[:, LANES * h:LANES * (h + 1)] = (kv[:, LANES * h:LANES * (h + 1)] + k_rope).astype(BF16)
    vb_ref[...] = kv[:, GROUP_HEADS * LANES:].astype(BF16)

    log_f = _log_sigmoid(ckv_gate[:, LANES:] + bf_ref[...]) * LOG2E
    lf = jnp.transpose(log_f)[0:8, :]
    hi = lf.astype(BF16)
    lo = (lf - hi.astype(F32)).astype(BF16)
    tri = tri_ref[...]
    local = _dot(hi, tri) + _dot(lo, tri)
    carry = jnp.where(i % tiles_per_seq == 0, 0.0, carry_ref[:, 0:1])
    cum = local + carry
    carry_ref[...] = jnp.broadcast_to(cum[:, tm - 1:tm], carry_ref.shape)
    tk = cum_ref.shape[-1]
    for r in range(tm // tk):
        cum_ref[0, r] = cum[:, tk * r:tk * (r + 1)]


def _in_proj(x2, w_in_p, wq_p, wkv_p, g_q, g_kv, b_f, tc, ts, tri, *, B, S):
    M = x2.shape[0]
    tm = TM_IN
    tiles_per_seq = S // tm
    row = lambda i: (i, 0)
    const = lambda i: (0, 0)
    pos = lambda i: (i % tiles_per_seq, 0)
    bf = lambda w: jax.ShapeDtypeStruct((M, w), BF16)
    out_shape = (bf(512), bf(256), bf(256),
                 jax.ShapeDtypeStruct((B, S // TQ, 8, TQ), F32),
                 bf(512), bf(512), bf(256),
                 bf(512), bf(256), bf(256),
                 bf(512), bf(128), bf(128))
    ob = lambda w: pl.BlockSpec((tm, w), row)
    out_specs = (ob(512), ob(256), ob(256),
                 pl.BlockSpec((1, tm // TQ, 8, TQ), lambda i: (i // tiles_per_seq, i % tiles_per_seq, 0, 0)),
                 ob(512), ob(512), ob(256),
                 ob(512), ob(256), ob(256),
                 ob(512), ob(128), ob(128))
    in_specs = [
        pl.BlockSpec((tm, D_MODEL), row),
        pl.BlockSpec((D_MODEL, IN_PERM_WIDTH), const),
        pl.BlockSpec(wq_p.shape, const),
        pl.BlockSpec(wkv_p.shape, const),
        pl.BlockSpec((1, MLA_Q_RANK), const),
        pl.BlockSpec((1, MLA_KV_RANK), const),
        pl.BlockSpec((1, LANES), const),
        pl.BlockSpec((tm, LANES), pos),
        pl.BlockSpec((tm, LANES), pos),
        pl.BlockSpec((tm, tm), const),
    ]
    kern = functools.partial(_in_proj_kernel, tiles_per_seq=tiles_per_seq,
                             mla_scale=float((HEAD_DIM + MLA_ROPE) ** -0.5 * LOG2E))
    return pl.pallas_call(
        kern, out_shape=out_shape, grid=(M // tm,), in_specs=in_specs, out_specs=out_specs,
        scratch_shapes=[pltpu.VMEM((8, LANES), F32)],
        compiler_params=pltpu.CompilerParams(dimension_semantics=("arbitrary",),
                                             vmem_limit_bytes=VMEM_LIMIT),
        name="in_proj",
    )(x2, w_in_p, wq_p, wkv_p, g_q, g_kv, b_f, tc, ts, tri)


def _pair_norm_store(o_ref, heads, g_ref, rows=None):
    t = heads[0].shape[0]
    lo_half = lax.broadcasted_iota(jnp.int32, (t, LANES), 1) < HEAD_DIM
    pairs = [jnp.where(lo_half, heads[2 * p], heads[2 * p + 1]) for p in range(2)]
    ss = sum(jnp.sum(p * p, axis=-1, keepdims=True) for p in pairs)
    inv = lax.rsqrt(ss * (1.0 / GROUP_WIDTH) + 1e-6)
    for p in range(2):
        val = (pairs[p] * inv * g_ref[:, LANES * p:LANES * (p + 1)]).astype(o_ref.dtype)
        if rows is None:
            o_ref[:, LANES * p:LANES * (p + 1)] = val
        else:
            o_ref[rows, LANES * p:LANES * (p + 1)] = val


def _softmax_attn_kernel(*refs, k_cols, has_bias):
    if has_bias:
        q_ref, k_ref, v_ref, cum_ref, g_ref, o_ref, m_ref, l_ref, acc_ref = refs
    else:
        q_ref, k_ref, v_ref, g_ref, o_ref, m_ref, l_ref, acc_ref = refs
        cum_ref = None
    tq = q_ref.shape[0]
    tk = tq
    i = pl.program_id(1)
    m_ref[...] = jnp.full(m_ref.shape, NEG_INF, F32)
    l_ref[...] = jnp.zeros(l_ref.shape, F32)
    acc_ref[...] = jnp.zeros(acc_ref.shape, F32)

    def step(kb, masked):
        r0 = pl.multiple_of(kb * tk, tk)
        for h in range(GROUP_HEADS):
            kc = k_cols[h]
            vc = LANES * (h // 2)
            s = _nt_dot(q_ref[:, LANES * h:LANES * (h + 1)], k_ref[pl.ds(r0, tk), kc:kc + LANES])
            if has_bias:
                s = s - cum_ref[0, kb][h:h + 1, :]
            if masked:
                causal = (lax.broadcasted_iota(jnp.int32, (tq, tk), 1)
                          <= lax.broadcasted_iota(jnp.int32, (tq, tk), 0))
                s = jnp.where(causal, s, NEG_INF)
            m_prev = m_ref[h]
            m_next = jnp.maximum(m_prev, jnp.max(s, axis=-1, keepdims=True))
            alpha = jnp.exp2(m_prev - m_next)
            p = jnp.exp2(s - jnp.concatenate([m_next] * (tk // LANES), axis=1))
            l_ref[h] = alpha * l_ref[h] + jnp.sum(p, axis=-1, keepdims=True)
            m_ref[h] = m_next
            acc_ref[h] = alpha * acc_ref[h] + _dot(p.astype(BF16), v_ref[pl.ds(r0, tk), vc:vc + LANES])

    def body(kb, carry):
        step(kb, False)
        return carry

    lax.fori_loop(0, i, body, 0)
    step(i, True)
    heads = [acc_ref[h] / l_ref[h] for h in range(GROUP_HEADS)]
    _pair_norm_store(o_ref, heads, g_ref)


def _softmax_attn(q, k, v, cum, g, *, B, S, k_cols, name):
    M = q.shape[0]
    nq = S // TQ
    kw = k.shape[1]
    has_bias = cum is not None
    in_specs = [
        pl.BlockSpec((TQ, 4 * LANES), lambda b, i: (b * nq + i, 0)),
        pl.BlockSpec((S, kw), lambda b, i: (b, 0)),
        pl.BlockSpec((S, GROUP_WIDTH), lambda b, i: (b, 0)),
    ]
    args = [q, k, v]
    if has_bias:
        in_specs.append(pl.BlockSpec((1, nq, 8, TQ), lambda b, i: (b, 0, 0, 0)))
        args.append(cum)
    in_specs.append(pl.BlockSpec((1, GROUP_WIDTH), lambda b, i: (0, 0)))
    args.append(g)
    kern = functools.partial(_softmax_attn_kernel, k_cols=k_cols, has_bias=has_bias)
    return pl.pallas_call(
        kern, out_shape=jax.ShapeDtypeStruct((M, GROUP_WIDTH), BF16), grid=(B, nq),
        in_specs=in_specs,
        out_specs=pl.BlockSpec((TQ, GROUP_WIDTH), lambda b, i: (b * nq + i, 0)),
        scratch_shapes=[pltpu.VMEM((GROUP_HEADS, TQ, LANES), F32)] * 3,
        compiler_params=pltpu.CompilerParams(dimension_semantics=("arbitrary", "arbitrary"),
                                             vmem_limit_bytes=VMEM_LIMIT),
        name=name,
    )(*args)


def _stick_kernel(q_ref, k_ref, v_ref, tri_ref, g_ref, o_ref, rest_ref, acc_ref):
    tq = q_ref.shape[0]
    tk = tq
    tw = tri_ref.shape[0]
    i = pl.program_id(1)
    rest_ref[...] = jnp.zeros(rest_ref.shape, F32)
    acc_ref[...] = jnp.zeros(acc_ref.shape, F32)

    def block(kb, masked):
        r0 = pl.multiple_of(kb * tk, tk)
        tri = tri_ref[...]
        for h in range(GROUP_HEADS):
            c0 = LANES * (h // 2)
            z = _nt_dot(q_ref[:, LANES * h:LANES * (h + 1)], k_ref[pl.ds(r0, tk), c0:c0 + LANES])
            z_neg = jnp.minimum(z, 0.0)
            lg = jnp.log(1.0 + jnp.exp2(z_neg - jnp.maximum(z, 0.0))) * LOG2E
            log_b = z_neg - lg
            log_1mb = log_b - z
            if masked:
                strict = (lax.broadcasted_iota(jnp.int32, (tq, tk), 1)
                          < lax.broadcasted_iota(jnp.int32, (tq, tk), 0))
                log_1mb = jnp.where(strict, log_1mb, 0.0)
            rest = rest_ref[h]
            parts = [None] * (tk // tw)
            for c in reversed(range(tk // tw)):
                sub = log_1mb[:, tw * c:tw * (c + 1)]
                parts[c] = (_dot(sub.astype(BF16), tri)
                            + jnp.concatenate([rest] * (tw // LANES), axis=1))
                rest = rest + jnp.sum(sub, axis=-1, keepdims=True)
            rest_ref[h] = rest
            a = jnp.exp2(log_b + jnp.concatenate(parts, axis=1))
            if masked:
                a = jnp.where(strict, a, 0.0)
            acc_ref[h] = acc_ref[h] + _dot(a.astype(BF16), v_ref[pl.ds(r0, tk), c0:c0 + LANES])

    block(i, True)

    def body(t, carry):
        block(i - 1 - t, False)
        return carry

    lax.fori_loop(0, i, body, 0)
    _pair_norm_store(o_ref, [acc_ref[h] for h in range(GROUP_HEADS)], g_ref)


def _stick_attn(q, k, v, tri, g, *, B, S):
    M = q.shape[0]
    nq = S // TQ
    return pl.pallas_call(
        _stick_kernel, out_shape=jax.ShapeDtypeStruct((M, GROUP_WIDTH), BF16), grid=(B, nq),
        in_specs=[
            pl.BlockSpec((TQ, 4 * LANES), lambda b, i: (b * nq + i, 0)),
            pl.BlockSpec((S, GROUP_WIDTH), lambda b, i: (b, 0)),
            pl.BlockSpec((S, GROUP_WIDTH), lambda b, i: (b, 0)),
            pl.BlockSpec((TRI_W, TRI_W), lambda b, i: (0, 0)),
            pl.BlockSpec((1, GROUP_WIDTH), lambda b, i: (0, 0)),
        ],
        out_specs=pl.BlockSpec((TQ, GROUP_WIDTH), lambda b, i: (b * nq + i, 0)),
        scratch_shapes=[pltpu.VMEM((GROUP_HEADS, TQ, LANES), F32)] * 2,
        compiler_params=pltpu.CompilerParams(dimension_semantics=("arbitrary", "arbitrary"),
                                             vmem_limit_bytes=VMEM_LIMIT),
        name="stick_attn",
    )(q, k, v, tri, g)


def _swa_kernel(sink_ref, q_ref, kc_ref, kp_ref, vc_ref, vp_ref, g_ref, o_ref, *, slopes):
    tq = q_ref.shape[0]
    w = SWA_WINDOW
    i = pl.program_id(1)
    row = lax.broadcasted_iota(jnp.int32, (w, 2 * w), 0)
    col = lax.broadcasted_iota(jnp.int32, (w, 2 * w), 1)
    dist = row + w - col
    dist_f = dist.astype(F32)
    far = dist >= w
    ahead = dist < 0
    is_prev = col < w
    for r in range(tq // w):
        rows = slice(w * r, w * (r + 1))
        if r == 0:
            k2 = jnp.concatenate([kp_ref[...], kc_ref[0:w, :]], axis=0)
            v2 = jnp.concatenate([vp_ref[...], vc_ref[0:w, :]], axis=0)
        else:
            k2 = kc_ref[w * (r - 1):w * (r + 1), :]
            v2 = vc_ref[w * (r - 1):w * (r + 1), :]
        q4 = jnp.concatenate([q_ref[rows, LANES * s:LANES * (s + 1)] for s in range(GROUP_HEADS)], axis=0)
        s_all = _nt_dot(q4, k2)
        probs, dens = [], []
        for slot in range(GROUP_HEADS):
            head = slot // 2 + 2 * (slot % 2)
            sink = sink_ref[head] * LOG2E
            s = s_all[w * slot:w * (slot + 1), :] - (slopes[head] * LOG2E) * dist_f
            s = jnp.where(far, NEG_INF, jnp.where(ahead, NEG_INF, s))
            if r == 0:
                s = jnp.where(is_prev, jnp.where(i > 0, s, NEG_INF), s)
            m = jnp.maximum(jnp.max(s, axis=-1, keepdims=True), sink)
            p = jnp.exp2(s - m)
            dens.append(jnp.sum(p, axis=-1, keepdims=True) + jnp.exp2(sink - m))
            probs.append(p.astype(BF16))
        pv = _dot(jnp.concatenate(probs, axis=0), v2)
        heads = [pv[w * s:w * (s + 1), :] / dens[s] for s in range(GROUP_HEADS)]
        _pair_norm_store(o_ref, heads, g_ref, rows=rows)


def _swa_attn(sinks, q, k, v, g, *, B, S):
    M = q.shape[0]
    tq = TQ_SWA
    nq = S // tq
    per = tq // SWA_WINDOW
    slopes = tuple(float(2.0 ** (-8.0 * (h + 1) / GROUP_HEADS)) for h in range(GROUP_HEADS))
    cur = lambda b, i: (b * nq + i, 0)
    prev = lambda b, i: (b * nq * per + jnp.maximum(i * per - 1, 0), 0)
    kern = functools.partial(_swa_kernel, slopes=slopes)
    return pl.pallas_call(
        kern, out_shape=jax.ShapeDtypeStruct((M, GROUP_WIDTH), BF16), grid=(B, nq),
        in_specs=[
            pl.BlockSpec(memory_space=pltpu.SMEM),
            pl.BlockSpec((tq, 4 * LANES), cur),
            pl.BlockSpec((tq, LANES), cur),
            pl.BlockSpec((SWA_WINDOW, LANES), prev),
            pl.BlockSpec((tq, LANES), cur),
            pl.BlockSpec((SWA_WINDOW, LANES), prev),
            pl.BlockSpec((1, GROUP_WIDTH), lambda b, i: (0, 0)),
        ],
        out_specs=pl.BlockSpec((tq, GROUP_WIDTH), cur),
        compiler_params=pltpu.CompilerParams(dimension_semantics=("arbitrary", "arbitrary"),
                                             vmem_limit_bytes=VMEM_LIMIT),
        name="swa_attn",
    )(sinks, q, k, k, v, v, g)


def _layernorm(r, g, b):
    mu = jnp.mean(r, axis=-1, keepdims=True)
    d = r - mu
    var = jnp.mean(d * d, axis=-1, keepdims=True)
    return d * lax.rsqrt(var + 1e-5) * g + b


def _out_ffn_kernel(x_ref, ma_ref, mb_ref, mc_ref, md_ref, wo_ref, g1_ref, b1_ref,
                    wg_ref, wu_ref, wd_ref, g2_ref, b2_ref, o_ref, *, alpha):
    y = None
    for n, m_ref in enumerate((ma_ref, mb_ref, mc_ref, md_ref)):
        part = _dot(m_ref[...], wo_ref[GROUP_WIDTH * n:GROUP_WIDTH * (n + 1), :])
        y = part if y is None else y + part
    x1 = _layernorm(alpha * x_ref[...] + y, g1_ref[...], b1_ref[...])
    x1b = x1.astype(BF16)
    f = None
    for c in range(D_FF // F_CHUNK):
        cols = slice(F_CHUNK * c, F_CHUNK * (c + 1))
        gate = _dot(x1b, wg_ref[:, cols])
        up = _dot(x1b, wu_ref[:, cols])
        hid = (gate * (1.0 / (1.0 + jnp.exp(-gate))) * up).astype(BF16)
        part = _dot(hid, wd_ref[cols, :])
        f = part if f is None else f + part
    o_ref[...] = _layernorm(alpha * x1 + f, g2_ref[...], b2_ref[...])


def _out_ffn(x2, mixes, wo, g1, b1, wg, wu, wd, g2, b2, *, alpha):
    M = x2.shape[0]
    tm = TM_OUT
    row = lambda i: (i, 0)
    const = lambda i: (0, 0)
    resident = lambda shape: pl.BlockSpec(shape, const, pipeline_mode=pl.Buffered(1))
    vec = pl.BlockSpec((1, D_MODEL), const)
    in_specs = [pl.BlockSpec((tm, D_MODEL), row)]
    in_specs += [pl.BlockSpec((tm, GROUP_WIDTH), row)] * 4
    in_specs += [resident((D_MODEL, D_MODEL)), vec, vec,
                 resident((D_MODEL, D_FF)), resident((D_MODEL, D_FF)), resident((D_FF, D_MODEL)),
                 vec, vec]
    kern = functools.partial(_out_ffn_kernel, alpha=alpha)
    return pl.pallas_call(
        kern, out_shape=jax.ShapeDtypeStruct((M, D_MODEL), F32), grid=(M // tm,),
        in_specs=in_specs, out_specs=pl.BlockSpec((tm, D_MODEL), row),
        compiler_params=pltpu.CompilerParams(dimension_semantics=("arbitrary",),
                                             vmem_limit_bytes=VMEM_LIMIT),
        name="out_ffn",
    )(x2, *mixes, wo, g1, b1, wg, wu, wd, g2, b2)


def kernel(x, w_in, fox_b_f, mla_g_q, mla_g_kv, mla_w_uq, mla_w_ukv, swa_sinks, mix_g, w_o,
           ln1_g, ln1_b, w_gate, w_up, w_down, ln2_g, ln2_b):
    B, S, D = x.shape
    depth = w_in.shape[0]
    assert D == D_MODEL and S % TM_IN == 0 and S % TQ_SWA == 0 and (B * S) % TM_OUT == 0

    w_in_p = _gather_cols(w_in, _in_proj_columns()).astype(BF16)
    wq_p = _gather_cols(mla_w_uq, _mla_q_columns()).astype(BF16)
    wkv_p = _gather_cols(mla_w_ukv, _mla_kv_columns()).astype(BF16)
    order = _mix_row_order()
    wo_p = jnp.take(w_o, jnp.asarray(order, jnp.int32), axis=1).astype(BF16)
    mix_g_p = jnp.take(mix_g, jnp.asarray(order, jnp.int32), axis=1)
    wg_b, wu_b, wd_b = w_gate.astype(BF16), w_up.astype(BF16), w_down.astype(BF16)
    b_f = jnp.pad(fox_b_f.astype(F32), ((0, 0), (0, LANES - fox_b_f.shape[1])))
    tc, ts = _rope_slot_tables(S)
    idx_in = np.arange(TM_IN)
    tri_incl = jnp.asarray(idx_in[:, None] <= idx_in[None, :], BF16)
    idx_q = np.arange(TRI_W)
    tri_after = jnp.asarray(idx_q[:, None] > idx_q[None, :], BF16)

    k_cols_packed = tuple(LANES * (h // 2) for h in range(GROUP_HEADS))
    k_cols_slots = tuple(LANES * h for h in range(GROUP_HEADS))

    x2 = x.reshape(B * S, D)
    for l in range(depth):
        (qa, ka, va, cum, qb, kb, vb, qc, kc, vc, qd, kd, vd) = _in_proj(
            x2, w_in_p[l], wq_p[l], wkv_p[l], mla_g_q[l][None, :], mla_g_kv[l][None, :],
            b_f[l][None, :], tc, ts, tri_incl, B=B, S=S)
        g = mix_g_p[l][None, :]
        mix_a = _softmax_attn(qa, ka, va, cum, g[:, 0:256], B=B, S=S, k_cols=k_cols_packed, name="fox_attn")
        mix_b = _softmax_attn(qb, kb, vb, None, g[:, 256:512], B=B, S=S, k_cols=k_cols_slots, name="mla_attn")
        mix_c = _stick_attn(qc, kc, vc, tri_after, g[:, 512:768], B=B, S=S)
        mix_d = _swa_attn(swa_sinks[l].astype(F32), qd, kd, vd, g[:, 768:1024], B=B, S=S)
        x2 = _out_ffn(x2, (mix_a, mix_b, mix_c, mix_d), wo_p[l], ln1_g[l][None, :], ln1_b[l][None, :],
                      wg_b[l], wu_b[l], wd_b[l], ln2_g[l][None, :], ln2_b[l][None, :], alpha=ALPHA)
    return x2.reshape(B, S, D)
```

```python
import functools

import numpy as np
import jax
import jax.numpy as jnp
from jax import lax
from jax.experimental import pallas as pl
from jax.experimental.pallas import tpu as pltpu

F32 = jnp.float32
BF16 = jnp.bfloat16

D_MODEL = 1024
HEAD_DIM = 64
GROUP_HEADS = 4
GROUP_WIDTH = GROUP_HEADS * HEAD_DIM
MLA_Q_RANK = 256
MLA_KV_RANK = 128
MLA_ROPE = 32
SWA_WINDOW = 128
D_FF = 2816
NEG_INF = -1e30
DEPTH = 4
ALPHA = (2.0 * DEPTH) ** 0.25
ROPE_THETA = 10000.0

LANES = 128
VMEM_LIMIT = 52 * 1024 * 1024

C_FQ, C_FK, C_FV = 0, 256, 512
C_CQ = 768
C_SQ, C_SK, C_SV = 1024, 1280, 1536
C_WQ, C_WK, C_WV = 1792, 2048, 2176
C_KR1, C_KR2 = 2304, 2432
C_CKV, C_GATE = 2560, 2688
IN_PERM_WIDTH = 2816
LOG2E = 1.4426950408889634

TM_IN = 1024
TM_OUT = 512
TQ = 512
TRI_W = 256
TQ_SWA = 512
F_CHUNK = 512


def _nt_dot(a, b):
    return lax.dot_general(a, b, (((1,), (1,)), ((), ())), preferred_element_type=F32)


def _dot(a, b):
    return jnp.dot(a, b, preferred_element_type=F32)


def _log_sigmoid(x):
    return jnp.minimum(x, 0.0) - jnp.log(1.0 + jnp.exp(-jnp.abs(x)))


def _in_proj_columns():
    sizes = (256, 256, 256, 4, MLA_Q_RANK, MLA_KV_RANK, MLA_ROPE, 256, 256, 256, 256, 128, 128)
    starts = np.concatenate([[0], np.cumsum(sizes)[:-1]])
    (fq, fk, fv, fg, cq, ckv, kr, sq, sk, sv, wq, wk, wv) = [int(s) for s in starts]
    idx = np.full((IN_PERM_WIDTH,), -1, np.int64)

    def put(dst, src, n):
        idx[dst:dst + n] = np.arange(src, src + n)

    put(C_FQ, fq, 256); put(C_FK, fk, 256); put(C_FV, fv, 256)
    put(C_CQ, cq, MLA_Q_RANK); put(C_CKV, ckv, MLA_KV_RANK)
    put(C_SQ, sq, 256); put(C_SK, sk, 256); put(C_SV, sv, 256)
    for slot, head in enumerate((0, 2, 1, 3)):
        put(C_WQ + 64 * slot, wq + 64 * head, 64)
    put(C_WK, wk, 128); put(C_WV, wv, 128)
    half = MLA_ROPE // 2
    put(C_KR1 + 64, kr, MLA_ROPE)
    put(C_KR2 + 64, kr + half, half)
    put(C_KR2 + 64 + half, kr, half)
    put(C_GATE, fg, 4)
    return idx


def _gather_cols(w, idx):
    pieces, start = [], 0
    n = len(idx)
    while start < n:
        stop = start + 1
        if idx[start] < 0:
            while stop < n and idx[stop] < 0:
                stop += 1
            pieces.append(jnp.zeros(w.shape[:-1] + (stop - start,), w.dtype))
        else:
            while stop < n and idx[stop] == idx[stop - 1] + 1:
                stop += 1
            pieces.append(w[..., int(idx[start]):int(idx[stop - 1]) + 1])
        start = stop
    return jnp.concatenate(pieces, axis=-1)


def _mla_q_columns():
    half = MLA_ROPE // 2
    per = HEAD_DIM + MLA_ROPE
    idx = np.full((2 * GROUP_HEADS * LANES,), -1, np.int64)
    for h in range(GROUP_HEADS):
        idx[LANES * h:LANES * h + per] = np.arange(per * h, per * h + per)
        base = GROUP_HEADS * LANES + LANES * h + HEAD_DIM
        idx[base:base + half] = np.arange(per * h + HEAD_DIM + half, per * h + per)
        idx[base + half:base + 2 * half] = np.arange(per * h + HEAD_DIM, per * h + HEAD_DIM + half)
    return idx


def _mla_kv_columns():
    idx = np.full((GROUP_HEADS * LANES + GROUP_WIDTH,), -1, np.int64)
    for h in range(GROUP_HEADS):
        idx[LANES * h:LANES * h + HEAD_DIM] = np.arange(2 * HEAD_DIM * h, 2 * HEAD_DIM * h + HEAD_DIM)
        v0 = GROUP_HEADS * LANES + HEAD_DIM * h
        idx[v0:v0 + HEAD_DIM] = np.arange(2 * HEAD_DIM * h + HEAD_DIM, 2 * HEAD_DIM * (h + 1))
    return idx


def _mix_row_order():
    idx = np.arange(4 * GROUP_WIDTH)
    base = 3 * GROUP_WIDTH
    for slot, head in enumerate((0, 2, 1, 3)):
        idx[base + 64 * slot:base + 64 * (slot + 1)] = np.arange(base + 64 * head, base + 64 * (head + 1))
    return idx


def _rope_slot_tables(S):
    pos = jnp.arange(S, dtype=F32)
    inv = ROPE_THETA ** (-jnp.arange(0, MLA_ROPE, 2, dtype=F32) / MLA_ROPE)
    ang = pos[:, None] * inv[None, :]
    cos, sin = jnp.cos(ang), jnp.sin(ang)
    ones = jnp.ones((S, HEAD_DIM), F32)
    zeros32 = jnp.zeros((S, LANES - HEAD_DIM - MLA_ROPE), F32)
    tc = jnp.concatenate([ones, cos, cos, zeros32], axis=1)
    ts = jnp.concatenate([jnp.zeros((S, HEAD_DIM), F32), -sin, sin, zeros32], axis=1)
    return tc, ts


def _in_proj_kernel(x_ref, w_ref, wq_ref, wkv_ref, gq_ref, gkv_ref, bf_ref, tc_ref, ts_ref, tri_ref,
                    qa_ref, ka_ref, va_ref, cum_ref, qb_ref, kb_ref, vb_ref,
                    qc_ref, kc_ref, vc_ref, qd_ref, kd_ref, vd_ref, carry_ref,
                    *, tiles_per_seq, mla_scale):
    tm = x_ref.shape[0]
    i = pl.program_id(0)
    xb = x_ref[...].astype(BF16)

    def proj(c0, n):
        return _dot(xb, w_ref[:, c0:c0 + n])

    lo_half = lax.broadcasted_iota(jnp.int32, (tm, LANES), 1) < HEAD_DIM

    def store_slots(ref, packed, scale):
        for p in range(2):
            pair = packed[:, LANES * p:LANES * (p + 1)] * scale
            ref[:, 2 * LANES * p:2 * LANES * p + LANES] = jnp.where(lo_half, pair, 0.0).astype(BF16)
            ref[:, 2 * LANES * p + LANES:2 * LANES * (p + 1)] = jnp.where(lo_half, 0.0, pair).astype(BF16)

    head_scale = HEAD_DIM ** -0.5 * LOG2E
    store_slots(qa_ref, proj(C_FQ, 256), head_scale)
    ka_ref[...] = proj(C_FK, 256).astype(BF16)
    va_ref[...] = proj(C_FV, 256).astype(BF16)
    store_slots(qc_ref, proj(C_SQ, 256), head_scale)
    kc_ref[...] = proj(C_SK, 256).astype(BF16)
    vc_ref[...] = proj(C_SV, 256).astype(BF16)
    store_slots(qd_ref, proj(C_WQ, 256), head_scale)
    kv_d = proj(C_WK, 2 * LANES)
    kd_ref[...] = kv_d[:, :LANES].astype(BF16)
    vd_ref[...] = kv_d[:, LANES:].astype(BF16)

    tc = tc_ref[...]
    ts = ts_ref[...]
    cq = proj(C_CQ, MLA_Q_RANK)
    cqn = (cq * lax.rsqrt(jnp.mean(cq * cq, axis=-1, keepdims=True) + 1e-6) * gq_ref[...]).astype(BF16)
    q12 = _dot(cqn, wq_ref[...])
    for h in range(GROUP_HEADS):
        q1 = q12[:, LANES * h:LANES * (h + 1)]
        q2 = q12[:, LANES * (GROUP_HEADS + h):LANES * (GROUP_HEADS + h + 1)]
        qb_ref[:, LANES * h:LANES * (h + 1)] = ((q1 * tc + q2 * ts) * mla_scale).astype(BF16)
    ckv_gate = proj(C_CKV, 2 * LANES)
    ckv = ckv_gate[:, :MLA_KV_RANK]
    ckvn = (ckv * lax.rsqrt(jnp.mean(ckv * ckv, axis=-1, keepdims=True) + 1e-6) * gkv_ref[...]).astype(BF16)
    kv = _dot(ckvn, wkv_ref[...])
    kr12 = proj(C_KR1, 2 * LANES)
    k_rope = kr12[:, :LANES] * tc + kr12[:, LANES:] * ts
    for h in range(GROUP_HEADS):
        kb_ref[:, LANES * h:LANES * (h + 1)] = (kv[:, LANES * h:LANES * (h + 1)] + k_rope).astype(BF16)
    vb_ref[...] = kv[:, GROUP_HEADS * LANES:].astype(BF16)

    log_f = _log_sigmoid(ckv_gate[:, LANES:] + bf_ref[...]) * LOG2E
    lf = jnp.transpose(log_f)[0:8, :]
    hi = lf.astype(BF16)
    lo = (lf - hi.astype(F32)).astype(BF16)
    tri = tri_ref[...]
    local = _dot(hi, tri) + _dot(lo, tri)
    carry = jnp.where(i % tiles_per_seq == 0, 0.0, carry_ref[:, 0:1])
    cum = local + carry
    carry_ref[...] = jnp.broadcast_to(cum[:, tm - 1:tm], carry_ref.shape)
    tk = cum_ref.shape[-1]
    for r in range(tm // tk):
        cum_ref[0, r] = cum[:, tk * r:tk * (r + 1)]


def _in_proj(x2, w_in_p, wq_p, wkv_p, g_q, g_kv, b_f, tc, ts, tri, *, B, S):
    M = x2.shape[0]
    tm = TM_IN
    tiles_per_seq = S // tm
    row = lambda i: (i, 0)
    const = lambda i: (0, 0)
    pos = lambda i: (i % tiles_per_seq, 0)
    bf = lambda w: jax.ShapeDtypeStruct((M, w), BF16)
    out_shape = (bf(512), bf(256), bf(256),
                 jax.ShapeDtypeStruct((B, S // TQ, 8, TQ), F32),
                 bf(512), bf(512), bf(256),
                 bf(512), bf(256), bf(256),
                 bf(512), bf(128), bf(128))
    ob = lambda w: pl.BlockSpec((tm, w), row)
    out_specs = (ob(512), ob(256), ob(256),
                 pl.BlockSpec((1, tm // TQ, 8, TQ), lambda i: (i // tiles_per_seq, i % tiles_per_seq, 0, 0)),
                 ob(512), ob(512), ob(256),
                 ob(512), ob(256), ob(256),
                 ob(512), ob(128), ob(128))
    in_specs = [
        pl.BlockSpec((tm, D_MODEL), row),
        pl.BlockSpec((D_MODEL, IN_PERM_WIDTH), const),
        pl.BlockSpec(wq_p.shape, const),
        pl.BlockSpec(wkv_p.shape, const),
        pl.BlockSpec((1, MLA_Q_RANK), const),
        pl.BlockSpec((1, MLA_KV_RANK), const),
        pl.BlockSpec((1, LANES), const),
        pl.BlockSpec((tm, LANES), pos),
        pl.BlockSpec((tm, LANES), pos),
        pl.BlockSpec((tm, tm), const),
    ]
    kern = functools.partial(_in_proj_kernel, tiles_per_seq=tiles_per_seq,
                             mla_scale=float((HEAD_DIM + MLA_ROPE) ** -0.5 * LOG2E))
    return pl.pallas_call(
        kern, out_shape=out_shape, grid=(M // tm,), in_specs=in_specs, out_specs=out_specs,
        scratch_shapes=[pltpu.VMEM((8, LANES), F32)],
        compiler_params=pltpu.CompilerParams(dimension_semantics=("arbitrary",),
                                             vmem_limit_bytes=VMEM_LIMIT),
        name="in_proj",
    )(x2, w_in_p, wq_p, wkv_p, g_q, g_kv, b_f, tc, ts, tri)


def _pair_norm_store(o_ref, heads, g_ref, rows=None):
    t = heads[0].shape[0]
    lo_half = lax.broadcasted_iota(jnp.int32, (t, LANES), 1) < HEAD_DIM
    pairs = [jnp.where(lo_half, heads[2 * p], heads[2 * p + 1]) for p in range(2)]
    ss = sum(jnp.sum(p * p, axis=-1, keepdims=True) for p in pairs)
    inv = lax.rsqrt(ss * (1.0 / GROUP_WIDTH) + 1e-6)
    for p in range(2):
        val = (pairs[p] * inv * g_ref[:, LANES * p:LANES * (p + 1)]).astype(o_ref.dtype)
        if rows is None:
            o_ref[:, LANES * p:LANES * (p + 1)] = val
        else:
            o_ref[rows, LANES * p:LANES * (p + 1)] = val


def _softmax_attn_kernel(*refs, k_cols, has_bias):
    if has_bias:
        q_ref, k_ref, v_ref, cum_ref, g_ref, o_ref, m_ref, l_ref, acc_ref = refs
    else:
        q_ref, k_ref, v_ref, g_ref, o_ref, m_ref, l_ref, acc_ref = refs
        cum_ref = None
    tq = q_ref.shape[0]
    tk = tq
    i = pl.program_id(1)
    m_ref[...] = jnp.full(m_ref.shape, NEG_INF, F32)
    l_ref[...] = jnp.zeros(l_ref.shape, F32)
    acc_ref[...] = jnp.zeros(acc_ref.shape, F32)

    half = tq // 2

    def scores(h, rows, key0, nk, kb, cum_cols):
        kc = k_cols[h]
        s = _nt_dot(q_ref[rows, LANES * h:LANES * (h + 1)], k_ref[pl.ds(key0, nk), kc:kc + LANES])
        if has_bias:
            s = s - cum_ref[0, kb][h:h + 1, cum_cols]
        return s

    def update(h, rows, s, key0):
        nk = s.shape[1]
        vc = LANES * (h // 2)
        m_prev = m_ref[h, rows, :]
        m_next = jnp.maximum(m_prev, jnp.max(s, axis=-1, keepdims=True))
        alpha = jnp.exp2(m_prev - m_next)
        p = jnp.exp2(s - jnp.concatenate([m_next] * (nk // LANES), axis=1))
        l_ref[h, rows, :] = alpha * l_ref[h, rows, :] + jnp.sum(p, axis=-1, keepdims=True)
        m_ref[h, rows, :] = m_next
        acc_ref[h, rows, :] = (alpha * acc_ref[h, rows, :]
                               + _dot(p.astype(BF16), v_ref[pl.ds(key0, nk), vc:vc + LANES]))

    every = slice(0, tq)
    lower = slice(half, tq)

    def body(kb, carry):
        r0 = pl.multiple_of(kb * tk, tk)
        for h in range(GROUP_HEADS):
            update(h, every, scores(h, every, r0, tk, kb, slice(0, tk)), r0)
        return carry

    lax.fori_loop(0, i, body, 0)

    r0 = pl.multiple_of(i * tk, tk)
    r1 = pl.multiple_of(i * tk + half, half)
    causal = (lax.broadcasted_iota(jnp.int32, (half, half), 1)
              <= lax.broadcasted_iota(jnp.int32, (half, half), 0))
    for h in range(GROUP_HEADS):
        s = scores(h, every, r0, half, i, slice(0, half))
        s = jnp.concatenate([jnp.where(causal, s[:half], NEG_INF), s[half:]], axis=0)
        update(h, every, s, r0)
        s = jnp.where(causal, scores(h, lower, r1, half, i, slice(half, tk)), NEG_INF)
        update(h, lower, s, r1)
    heads = [acc_ref[h] / l_ref[h] for h in range(GROUP_HEADS)]
    _pair_norm_store(o_ref, heads, g_ref)


def _softmax_attn(q, k, v, cum, g, *, B, S, k_cols, name):
    M = q.shape[0]
    nq = S // TQ
    kw = k.shape[1]
    has_bias = cum is not None
    in_specs = [
        pl.BlockSpec((TQ, 4 * LANES), lambda b, i: (b * nq + i, 0)),
        pl.BlockSpec((S, kw), lambda b, i: (b, 0)),
        pl.BlockSpec((S, GROUP_WIDTH), lambda b, i: (b, 0)),
    ]
    args = [q, k, v]
    if has_bias:
        in_specs.append(pl.BlockSpec((1, nq, 8, TQ), lambda b, i: (b, 0, 0, 0)))
        args.append(cum)
    in_specs.append(pl.BlockSpec((1, GROUP_WIDTH), lambda b, i: (0, 0)))
    args.append(g)
    kern = functools.partial(_softmax_attn_kernel, k_cols=k_cols, has_bias=has_bias)
    return pl.pallas_call(
        kern, out_shape=jax.ShapeDtypeStruct((M, GROUP_WIDTH), BF16), grid=(B, nq),
        in_specs=in_specs,
        out_specs=pl.BlockSpec((TQ, GROUP_WIDTH), lambda b, i: (b * nq + i, 0)),
        scratch_shapes=[pltpu.VMEM((GROUP_HEADS, TQ, LANES), F32)] * 3,
        compiler_params=pltpu.CompilerParams(dimension_semantics=("arbitrary", "arbitrary"),
                                             vmem_limit_bytes=VMEM_LIMIT),
        name=name,
    )(*args)


def _stick_kernel(q_ref, k_ref, v_ref, tri_ref, g_ref, o_ref, rest_ref, acc_ref):
    tq = q_ref.shape[0]
    tk = tq
    tw = tri_ref.shape[0]
    i = pl.program_id(1)
    rest_ref[...] = jnp.zeros(rest_ref.shape, F32)
    acc_ref[...] = jnp.zeros(acc_ref.shape, F32)

    half = tq // 2
    assert half % tw == 0
    every = slice(0, tq)
    lower = slice(half, tq)
    strict = (lax.broadcasted_iota(jnp.int32, (half, half), 1)
              < lax.broadcasted_iota(jnp.int32, (half, half), 0))

    def mask_rows(x, rows_masked, fill):
        if rows_masked == 0:
            return x
        top = jnp.where(strict, x[:half], fill)
        return top if rows_masked == x.shape[0] else jnp.concatenate([top, x[half:]], axis=0)

    def block(h, rows, key0, nk, rows_masked):
        c0 = LANES * (h // 2)
        tri = tri_ref[...]
        z = _nt_dot(q_ref[rows, LANES * h:LANES * (h + 1)], k_ref[pl.ds(key0, nk), c0:c0 + LANES])
        z_neg = jnp.minimum(z, 0.0)
        lg = jnp.log(1.0 + jnp.exp2(z_neg - jnp.maximum(z, 0.0))) * LOG2E
        log_b = z_neg - lg
        log_1mb = mask_rows(log_b - z, rows_masked, 0.0)
        rest = rest_ref[h, rows, :]
        parts = [None] * (nk // tw)
        for c in reversed(range(nk // tw)):
            sub = log_1mb[:, tw * c:tw * (c + 1)]
            parts[c] = (_dot(sub.astype(BF16), tri)
                        + jnp.concatenate([rest] * (tw // LANES), axis=1))
            rest = rest + jnp.sum(sub, axis=-1, keepdims=True)
        rest_ref[h, rows, :] = rest
        a = mask_rows(jnp.exp2(log_b + jnp.concatenate(parts, axis=1)), rows_masked, 0.0)
        acc_ref[h, rows, :] = (acc_ref[h, rows, :]
                               + _dot(a.astype(BF16), v_ref[pl.ds(key0, nk), c0:c0 + LANES]))

    r0 = pl.multiple_of(i * tk, tk)
    r1 = pl.multiple_of(i * tk + half, half)
    for h in range(GROUP_HEADS):
        block(h, lower, r1, half, half)
        block(h, every, r0, half, half)

    def body(t, carry):
        k0 = pl.multiple_of((i - 1 - t) * tk, tk)
        for h in range(GROUP_HEADS):
            block(h, every, k0, tk, 0)
        return carry

    lax.fori_loop(0, i, body, 0)
    _pair_norm_store(o_ref, [acc_ref[h] for h in range(GROUP_HEADS)], g_ref)


def _stick_attn(q, k, v, tri, g, *, B, S):
    M = q.shape[0]
    nq = S // TQ
    return pl.pallas_call(
        _stick_kernel, out_shape=jax.ShapeDtypeStruct((M, GROUP_WIDTH), BF16), grid=(B, nq),
        in_specs=[
            pl.BlockSpec((TQ, 4 * LANES), lambda b, i: (b * nq + i, 0)),
            pl.BlockSpec((S, GROUP_WIDTH), lambda b, i: (b, 0)),
            pl.BlockSpec((S, GROUP_WIDTH), lambda b, i: (b, 0)),
            pl.BlockSpec((TRI_W, TRI_W), lambda b, i: (0, 0)),
            pl.BlockSpec((1, GROUP_WIDTH), lambda b, i: (0, 0)),
        ],
        out_specs=pl.BlockSpec((TQ, GROUP_WIDTH), lambda b, i: (b * nq + i, 0)),
        scratch_shapes=[pltpu.VMEM((GROUP_HEADS, TQ, LANES), F32)] * 2,
        compiler_params=pltpu.CompilerParams(dimension_semantics=("arbitrary", "arbitrary"),
                                             vmem_limit_bytes=VMEM_LIMIT),
        name="stick_attn",
    )(q, k, v, tri, g)


def _swa_kernel(sink_ref, q_ref, kc_ref, kp_ref, vc_ref, vp_ref, g_ref, o_ref, *, slopes):
    tq = q_ref.shape[0]
    w = SWA_WINDOW
    i = pl.program_id(1)
    row = lax.broadcasted_iota(jnp.int32, (w, 2 * w), 0)
    col = lax.broadcasted_iota(jnp.int32, (w, 2 * w), 1)
    dist = row + w - col
    dist_f = dist.astype(F32)
    far = dist >= w
    ahead = dist < 0
    is_prev = col < w
    for r in range(tq // w):
        rows = slice(w * r, w * (r + 1))
        if r == 0:
            k2 = jnp.concatenate([kp_ref[...], kc_ref[0:w, :]], axis=0)
            v2 = jnp.concatenate([vp_ref[...], vc_ref[0:w, :]], axis=0)
        else:
            k2 = kc_ref[w * (r - 1):w * (r + 1), :]
            v2 = vc_ref[w * (r - 1):w * (r + 1), :]
        q4 = jnp.concatenate([q_ref[rows, LANES * s:LANES * (s + 1)] for s in range(GROUP_HEADS)], axis=0)
        s_all = _nt_dot(q4, k2)
        probs, dens = [], []
        for slot in range(GROUP_HEADS):
            head = slot // 2 + 2 * (slot % 2)
            sink = sink_ref[head] * LOG2E
            s = s_all[w * slot:w * (slot + 1), :] - (slopes[head] * LOG2E) * dist_f
            s = jnp.where(far, NEG_INF, jnp.where(ahead, NEG_INF, s))
            if r == 0:
                s = jnp.where(is_prev, jnp.where(i > 0, s, NEG_INF), s)
            m = jnp.maximum(jnp.max(s, axis=-1, keepdims=True), sink)
            p = jnp.exp2(s - m)
            dens.append(jnp.sum(p, axis=-1, keepdims=True) + jnp.exp2(sink - m))
            probs.append(p.astype(BF16))
        pv = _dot(jnp.concatenate(probs, axis=0), v2)
        heads = [pv[w * s:w * (s + 1), :] / dens[s] for s in range(GROUP_HEADS)]
        _pair_norm_store(o_ref, heads, g_ref, rows=rows)


def _swa_attn(sinks, q, k, v, g, *, B, S):
    M = q.shape[0]
    tq = TQ_SWA
    nq = S // tq
    per = tq // SWA_WINDOW
    slopes = tuple(float(2.0 ** (-8.0 * (h + 1) / GROUP_HEADS)) for h in range(GROUP_HEADS))
    cur = lambda b, i: (b * nq + i, 0)
    prev = lambda b, i: (b * nq * per + jnp.maximum(i * per - 1, 0), 0)
    kern = functools.partial(_swa_kernel, slopes=slopes)
    return pl.pallas_call(
        kern, out_shape=jax.ShapeDtypeStruct((M, GROUP_WIDTH), BF16), grid=(B, nq),
        in_specs=[
            pl.BlockSpec(memory_space=pltpu.SMEM),
            pl.BlockSpec((tq, 4 * LANES), cur),
            pl.BlockSpec((tq, LANES), cur),
            pl.BlockSpec((SWA_WINDOW, LANES), prev),
            pl.BlockSpec((tq, LANES), cur),
            pl.BlockSpec((SWA_WINDOW, LANES), prev),
            pl.BlockSpec((1, GROUP_WIDTH), lambda b, i: (0, 0)),
        ],
        out_specs=pl.BlockSpec((tq, GROUP_WIDTH), cur),
        compiler_params=pltpu.CompilerParams(dimension_semantics=("arbitrary", "arbitrary"),
                                             vmem_limit_bytes=VMEM_LIMIT),
        name="swa_attn",
    )(sinks, q, k, k, v, v, g)


def _layernorm(r, g, b):
    mu = jnp.mean(r, axis=-1, keepdims=True)
    d = r - mu
    var = jnp.mean(d * d, axis=-1, keepdims=True)
    return d * lax.rsqrt(var + 1e-5) * g + b


def _out_ffn_kernel(x_ref, ma_ref, mb_ref, mc_ref, md_ref, wo_ref, g1_ref, b1_ref,
                    wg_ref, wu_ref, wd_ref, g2_ref, b2_ref, o_ref, *, alpha):
    y = None
    for n, m_ref in enumerate((ma_ref, mb_ref, mc_ref, md_ref)):
        part = _dot(m_ref[...], wo_ref[GROUP_WIDTH * n:GROUP_WIDTH * (n + 1), :])
        y = part if y is None else y + part
    x1 = _layernorm(alpha * x_ref[...] + y, g1_ref[...], b1_ref[...])
    x1b = x1.astype(BF16)
    f = None
    for c0 in range(0, D_FF, F_CHUNK):
        cols = slice(c0, min(c0 + F_CHUNK, D_FF))
        gate = _dot(x1b, wg_ref[:, cols])
        up = _dot(x1b, wu_ref[:, cols])
        hid = (gate * (1.0 / (1.0 + jnp.exp(-gate))) * up).astype(BF16)
        part = _dot(hid, wd_ref[cols, :])
        f = part if f is None else f + part
    o_ref[...] = _layernorm(alpha * x1 + f, g2_ref[...], b2_ref[...])


def _out_ffn(x2, mixes, wo, g1, b1, wg, wu, wd, g2, b2, *, alpha):
    M = x2.shape[0]
    tm = TM_OUT
    row = lambda i: (i, 0)
    const = lambda i: (0, 0)
    resident = lambda shape: pl.BlockSpec(shape, const, pipeline_mode=pl.Buffered(1))
    vec = pl.BlockSpec((1, D_MODEL), const)
    in_specs = [pl.BlockSpec((tm, D_MODEL), row)]
    in_specs += [pl.BlockSpec((tm, GROUP_WIDTH), row)] * 4
    in_specs += [resident((D_MODEL, D_MODEL)), vec, vec,
                 resident((D_MODEL, D_FF)), resident((D_MODEL, D_FF)), resident((D_FF, D_MODEL)),
                 vec, vec]
    kern = functools.partial(_out_ffn_kernel, alpha=alpha)
    return pl.pallas_call(
        kern, out_shape=jax.ShapeDtypeStruct((M, D_MODEL), F32), grid=(M // tm,),
        in_specs=in_specs, out_specs=pl.BlockSpec((tm, D_MODEL), row),
        compiler_params=pltpu.CompilerParams(dimension_semantics=("arbitrary",),
                                             vmem_limit_bytes=VMEM_LIMIT),
        name="out_ffn",
    )(x2, *mixes, wo, g1, b1, wg, wu, wd, g2, b2)


def kernel(x, w_in, fox_b_f, mla_g_q, mla_g_kv, mla_w_uq, mla_w_ukv, swa_sinks, mix_g, w_o,
           ln1_g, ln1_b, w_gate, w_up, w_down, ln2_g, ln2_b):
    B, S, D = x.shape
    depth = w_in.shape[0]
    assert D == D_MODEL and S % TM_IN == 0 and S % TQ_SWA == 0 and (B * S) % TM_OUT == 0

    w_in_p = _gather_cols(w_in, _in_proj_columns()).astype(BF16)
    wq_p = _gather_cols(mla_w_uq, _mla_q_columns()).astype(BF16)
    wkv_p = _gather_cols(mla_w_ukv, _mla_kv_columns()).astype(BF16)
    order = _mix_row_order()
    wo_p = jnp.take(w_o, jnp.asarray(order, jnp.int32), axis=1).astype(BF16)
    mix_g_p = jnp.take(mix_g, jnp.asarray(order, jnp.int32), axis=1)
    wg_b, wu_b, wd_b = w_gate.astype(BF16), w_up.astype(BF16), w_down.astype(BF16)
    b_f = jnp.pad(fox_b_f.astype(F32), ((0, 0), (0, LANES - fox_b_f.shape[1])))
    tc, ts = _rope_slot_tables(S)
    idx_in = np.arange(TM_IN)
    tri_incl = jnp.asarray(idx_in[:, None] <= idx_in[None, :], BF16)
    idx_q = np.arange(TRI_W)
    tri_after = jnp.asarray(idx_q[:, None] > idx_q[None, :], BF16)

    k_cols_packed = tuple(LANES * (h // 2) for h in range(GROUP_HEADS))
    k_cols_slots = tuple(LANES * h for h in range(GROUP_HEADS))

    x2 = x.reshape(B * S, D)
    for l in range(depth):
        (qa, ka, va, cum, qb, kb, vb, qc, kc, vc, qd, kd, vd) = _in_proj(
            x2, w_in_p[l], wq_p[l], wkv_p[l], mla_g_q[l][None, :], mla_g_kv[l][None, :],
            b_f[l][None, :], tc, ts, tri_incl, B=B, S=S)
        g = mix_g_p[l][None, :]
        mix_a = _softmax_attn(qa, ka, va, cum, g[:, 0:256], B=B, S=S, k_cols=k_cols_packed, name="fox_attn")
        mix_b = _softmax_attn(qb, kb, vb, None, g[:, 256:512], B=B, S=S, k_cols=k_cols_slots, name="mla_attn")
        mix_c = _stick_attn(qc, kc, vc, tri_after, g[:, 512:768], B=B, S=S)
        mix_d = _swa_attn(swa_sinks[l].astype(F32), qd, kd, vd, g[:, 768:1024], B=B, S=S)
        x2 = _out_ffn(x2, (mix_a, mix_b, mix_c, mix_d), wo_p[l], ln1_g[l][None, :], ln1_b[l][None, :],
                      wg_b[l], wu_b[l], wd_b[l], ln2_g[l][None, :], ln2_b[l][None, :], alpha=ALPHA)
    return x2.reshape(B, S, D)
```

```python
import functools

import numpy as np
import jax
import jax.numpy as jnp
from jax import lax
from jax.experimental import pallas as pl
from jax.experimental.pallas import tpu as pltpu

F32 = jnp.float32
BF16 = jnp.bfloat16

D_MODEL = 1024
HEAD_DIM = 64
GROUP_HEADS = 4
GROUP_WIDTH = GROUP_HEADS * HEAD_DIM
MLA_Q_RANK = 256
MLA_KV_RANK = 128
MLA_ROPE = 32
SWA_WINDOW = 128
D_FF = 2816
NEG_INF = -1e30
DEPTH = 4
ALPHA = (2.0 * DEPTH) ** 0.25
ROPE_THETA = 10000.0

LANES = 128
VMEM_LIMIT = 52 * 1024 * 1024

C_FQ, C_FK, C_FV = 0, 256, 512
C_CQ = 768
C_SQ, C_SK, C_SV = 1024, 1280, 1536
C_WQ, C_WK, C_WV = 1792, 2048, 2176
C_KR1, C_KR2 = 2304, 2432
C_CKV, C_GATE = 2560, 2688
IN_PERM_WIDTH = 2816
LOG2E = 1.4426950408889634

TM_IN = 1024
TM_OUT = 512
TQ = 512
TRI_W = 256
TQ_SWA = 512
F_CHUNK = 512


def _nt_dot(a, b):
    return lax.dot_general(a, b, (((1,), (1,)), ((), ())), preferred_element_type=F32)


def _dot(a, b):
    return jnp.dot(a, b, preferred_element_type=F32)


def _log_sigmoid(x):
    return jnp.minimum(x, 0.0) - jnp.log(1.0 + jnp.exp(-jnp.abs(x)))


def _in_proj_columns():
    sizes = (256, 256, 256, 4, MLA_Q_RANK, MLA_KV_RANK, MLA_ROPE, 256, 256, 256, 256, 128, 128)
    starts = np.concatenate([[0], np.cumsum(sizes)[:-1]])
    (fq, fk, fv, fg, cq, ckv, kr, sq, sk, sv, wq, wk, wv) = [int(s) for s in starts]
    idx = np.full((IN_PERM_WIDTH,), -1, np.int64)

    def put(dst, src, n):
        idx[dst:dst + n] = np.arange(src, src + n)

    put(C_FQ, fq, 256); put(C_FK, fk, 256); put(C_FV, fv, 256)
    put(C_CQ, cq, MLA_Q_RANK); put(C_CKV, ckv, MLA_KV_RANK)
    put(C_SQ, sq, 256); put(C_SK, sk, 256); put(C_SV, sv, 256)
    for slot, head in enumerate((0, 2, 1, 3)):
        put(C_WQ + 64 * slot, wq + 64 * head, 64)
    put(C_WK, wk, 128); put(C_WV, wv, 128)
    half = MLA_ROPE // 2
    put(C_KR1 + 64, kr, MLA_ROPE)
    put(C_KR2 + 64, kr + half, half)
    put(C_KR2 + 64 + half, kr, half)
    put(C_GATE, fg, 4)
    return idx


def _gather_cols(w, idx, axis=-1):
    axis = axis % w.ndim
    pieces, start = [], 0
    n = len(idx)
    while start < n:
        stop = start + 1
        if idx[start] < 0:
            while stop < n and idx[stop] < 0:
                stop += 1
            shape = w.shape[:axis] + (stop - start,) + w.shape[axis + 1:]
            pieces.append(jnp.zeros(shape, w.dtype))
        else:
            while stop < n and idx[stop] == idx[stop - 1] + 1:
                stop += 1
            pieces.append(lax.slice_in_dim(w, int(idx[start]), int(idx[stop - 1]) + 1, axis=axis))
        start = stop
    return jnp.concatenate(pieces, axis=axis)


def _mla_q_columns():
    half = MLA_ROPE // 2
    per = HEAD_DIM + MLA_ROPE
    idx = np.full((2 * GROUP_HEADS * LANES,), -1, np.int64)
    for h in range(GROUP_HEADS):
        idx[LANES * h:LANES * h + per] = np.arange(per * h, per * h + per)
        base = GROUP_HEADS * LANES + LANES * h + HEAD_DIM
        idx[base:base + half] = np.arange(per * h + HEAD_DIM + half, per * h + per)
        idx[base + half:base + 2 * half] = np.arange(per * h + HEAD_DIM, per * h + HEAD_DIM + half)
    return idx


def _mla_kv_columns():
    idx = np.full((GROUP_HEADS * LANES + GROUP_WIDTH,), -1, np.int64)
    for h in range(GROUP_HEADS):
        idx[LANES * h:LANES * h + HEAD_DIM] = np.arange(2 * HEAD_DIM * h, 2 * HEAD_DIM * h + HEAD_DIM)
        v0 = GROUP_HEADS * LANES + HEAD_DIM * h
        idx[v0:v0 + HEAD_DIM] = np.arange(2 * HEAD_DIM * h + HEAD_DIM, 2 * HEAD_DIM * (h + 1))
    return idx


def _mix_row_order():
    idx = np.arange(4 * GROUP_WIDTH)
    base = 3 * GROUP_WIDTH
    for slot, head in enumerate((0, 2, 1, 3)):
        idx[base + 64 * slot:base + 64 * (slot + 1)] = np.arange(base + 64 * head, base + 64 * (head + 1))
    return idx


def _rope_slot_tables(S):
    pos = jnp.arange(S, dtype=F32)
    inv = ROPE_THETA ** (-jnp.arange(0, MLA_ROPE, 2, dtype=F32) / MLA_ROPE)
    ang = pos[:, None] * inv[None, :]
    cos, sin = jnp.cos(ang), jnp.sin(ang)
    ones = jnp.ones((S, HEAD_DIM), F32)
    zeros32 = jnp.zeros((S, LANES - HEAD_DIM - MLA_ROPE), F32)
    tc = jnp.concatenate([ones, cos, cos, zeros32], axis=1)
    ts = jnp.concatenate([jnp.zeros((S, HEAD_DIM), F32), -sin, sin, zeros32], axis=1)
    return tc, ts


def _in_proj_kernel(x_ref, w_ref, wq_ref, wkv_ref, gq_ref, gkv_ref, bf_ref, tc_ref, ts_ref, tri_ref,
                    qa_ref, ka_ref, va_ref, cum_ref, qb_ref, kb_ref, vb_ref,
                    qc_ref, kc_ref, vc_ref, qd_ref, kd_ref, vd_ref, carry_ref,
                    *, tiles_per_seq, mla_scale):
    tm = x_ref.shape[0]
    i = pl.program_id(0)
    xb = x_ref[...].astype(BF16)

    def proj(c0, n):
        return _dot(xb, w_ref[:, c0:c0 + n])

    lo_half = lax.broadcasted_iota(jnp.int32, (tm, LANES), 1) < HEAD_DIM

    def store_slots(ref, packed, scale):
        for p in range(2):
            pair = packed[:, LANES * p:LANES * (p + 1)] * scale
            ref[:, 2 * LANES * p:2 * LANES * p + LANES] = jnp.where(lo_half, pair, 0.0).astype(BF16)
            ref[:, 2 * LANES * p + LANES:2 * LANES * (p + 1)] = jnp.where(lo_half, 0.0, pair).astype(BF16)

    head_scale = HEAD_DIM ** -0.5 * LOG2E
    store_slots(qa_ref, proj(C_FQ, 256), head_scale)
    ka_ref[...] = proj(C_FK, 256).astype(BF16)
    va_ref[...] = proj(C_FV, 256).astype(BF16)
    store_slots(qc_ref, proj(C_SQ, 256), head_scale)
    kc_ref[...] = proj(C_SK, 256).astype(BF16)
    vc_ref[...] = proj(C_SV, 256).astype(BF16)
    store_slots(qd_ref, proj(C_WQ, 256), head_scale)
    kv_d = proj(C_WK, 2 * LANES)
    kd_ref[...] = kv_d[:, :LANES].astype(BF16)
    vd_ref[...] = kv_d[:, LANES:].astype(BF16)

    tc = tc_ref[...]
    ts = ts_ref[...]
    cq = proj(C_CQ, MLA_Q_RANK)
    cqn = (cq * lax.rsqrt(jnp.mean(cq * cq, axis=-1, keepdims=True) + 1e-6) * gq_ref[...]).astype(BF16)
    q12 = _dot(cqn, wq_ref[...])
    for h in range(GROUP_HEADS):
        q1 = q12[:, LANES * h:LANES * (h + 1)]
        q2 = q12[:, LANES * (GROUP_HEADS + h):LANES * (GROUP_HEADS + h + 1)]
        qb_ref[:, LANES * h:LANES * (h + 1)] = ((q1 * tc + q2 * ts) * mla_scale).astype(BF16)
    ckv_gate = proj(C_CKV, 2 * LANES)
    ckv = ckv_gate[:, :MLA_KV_RANK]
    ckvn = (ckv * lax.rsqrt(jnp.mean(ckv * ckv, axis=-1, keepdims=True) + 1e-6) * gkv_ref[...]).astype(BF16)
    kv = _dot(ckvn, wkv_ref[...])
    kr12 = proj(C_KR1, 2 * LANES)
    k_rope = kr12[:, :LANES] * tc + kr12[:, LANES:] * ts
    for h in range(GROUP_HEADS):
        kb_ref[:, LANES * h:LANES * (h + 1)] = (kv[:, LANES * h:LANES * (h + 1)] + k_rope).astype(BF16)
    vb_ref[...] = kv[:, GROUP_HEADS * LANES:].astype(BF16)

    log_f = _log_sigmoid(ckv_gate[:, LANES:] + bf_ref[...]) * LOG2E
    lf = jnp.transpose(log_f)[0:8, :]
    hi = lf.astype(BF16)
    lo = (lf - hi.astype(F32)).astype(BF16)
    tri = tri_ref[...]
    local = _dot(hi, tri) + _dot(lo, tri)
    carry = jnp.where(i % tiles_per_seq == 0, 0.0, carry_ref[:, 0:1])
    cum = local + carry
    carry_ref[...] = jnp.broadcast_to(cum[:, tm - 1:tm], carry_ref.shape)
    tk = cum_ref.shape[-1]
    for r in range(tm // tk):
        cum_ref[0, r] = cum[:, tk * r:tk * (r + 1)]


def _in_proj(x2, w_in_p, wq_p, wkv_p, g_q, g_kv, b_f, tc, ts, tri, *, B, S, layer):
    M = x2.shape[0]
    tm = TM_IN
    tiles_per_seq = S // tm
    row = lambda i: (i, 0)
    const = lambda i: (0, 0)
    stacked = lambda w: pl.BlockSpec((None,) + w.shape[1:], lambda i: (layer, 0, 0))
    pos = lambda i: (i % tiles_per_seq, 0)
    bf = lambda w: jax.ShapeDtypeStruct((M, w), BF16)
    out_shape = (bf(512), bf(256), bf(256),
                 jax.ShapeDtypeStruct((B, S // TQ, 8, TQ), F32),
                 bf(512), bf(512), bf(256),
                 bf(512), bf(256), bf(256),
                 bf(512), bf(128), bf(128))
    ob = lambda w: pl.BlockSpec((tm, w), row)
    out_specs = (ob(512), ob(256), ob(256),
                 pl.BlockSpec((1, tm // TQ, 8, TQ), lambda i: (i // tiles_per_seq, i % tiles_per_seq, 0, 0)),
                 ob(512), ob(512), ob(256),
                 ob(512), ob(256), ob(256),
                 ob(512), ob(128), ob(128))
    in_specs = [
        pl.BlockSpec((tm, D_MODEL), row),
        stacked(w_in_p),
        stacked(wq_p),
        stacked(wkv_p),
        pl.BlockSpec((1, MLA_Q_RANK), const),
        pl.BlockSpec((1, MLA_KV_RANK), const),
        pl.BlockSpec((1, LANES), const),
        pl.BlockSpec((tm, LANES), pos),
        pl.BlockSpec((tm, LANES), pos),
        pl.BlockSpec((tm, tm), const),
    ]
    kern = functools.partial(_in_proj_kernel, tiles_per_seq=tiles_per_seq,
                             mla_scale=float((HEAD_DIM + MLA_ROPE) ** -0.5 * LOG2E))
    return pl.pallas_call(
        kern, out_shape=out_shape, grid=(M // tm,), in_specs=in_specs, out_specs=out_specs,
        scratch_shapes=[pltpu.VMEM((8, LANES), F32)],
        compiler_params=pltpu.CompilerParams(dimension_semantics=("arbitrary",),
                                             vmem_limit_bytes=VMEM_LIMIT),
        name="in_proj",
    )(x2, w_in_p, wq_p, wkv_p, g_q, g_kv, b_f, tc, ts, tri)


def _pair_norm_store(o_ref, heads, g_ref, rows=None):
    t = heads[0].shape[0]
    lo_half = lax.broadcasted_iota(jnp.int32, (t, LANES), 1) < HEAD_DIM
    pairs = [jnp.where(lo_half, heads[2 * p], heads[2 * p + 1]) for p in range(2)]
    ss = sum(jnp.sum(p * p, axis=-1, keepdims=True) for p in pairs)
    inv = lax.rsqrt(ss * (1.0 / GROUP_WIDTH) + 1e-6)
    for p in range(2):
        val = (pairs[p] * inv * g_ref[:, LANES * p:LANES * (p + 1)]).astype(o_ref.dtype)
        if rows is None:
            o_ref[:, LANES * p:LANES * (p + 1)] = val
        else:
            o_ref[rows, LANES * p:LANES * (p + 1)] = val


def _softmax_attn_kernel(*refs, k_cols, has_bias):
    if has_bias:
        q_ref, k_ref, v_ref, cum_ref, g_ref, o_ref, m_ref, l_ref, acc_ref = refs
    else:
        q_ref, k_ref, v_ref, g_ref, o_ref, m_ref, l_ref, acc_ref = refs
        cum_ref = None
    tq = q_ref.shape[0]
    tk = tq
    i = pl.program_id(1)
    m_ref[...] = jnp.full(m_ref.shape, NEG_INF, F32)
    l_ref[...] = jnp.zeros(l_ref.shape, F32)
    acc_ref[...] = jnp.zeros(acc_ref.shape, F32)

    half = tq // 2

    def scores(h, rows, key0, nk, kb, cum_cols):
        kc = k_cols[h]
        s = _nt_dot(q_ref[rows, LANES * h:LANES * (h + 1)], k_ref[pl.ds(key0, nk), kc:kc + LANES])
        if has_bias:
            s = s - cum_ref[0, kb][h:h + 1, cum_cols]
        return s

    def update(h, rows, s, key0):
        nk = s.shape[1]
        vc = LANES * (h // 2)
        m_prev = m_ref[h, rows, :]
        m_next = jnp.maximum(m_prev, jnp.max(s, axis=-1, keepdims=True))
        alpha = jnp.exp2(m_prev - m_next)
        p = jnp.exp2(s - jnp.concatenate([m_next] * (nk // LANES), axis=1))
        l_ref[h, rows, :] = alpha * l_ref[h, rows, :] + jnp.sum(p, axis=-1, keepdims=True)
        m_ref[h, rows, :] = m_next
        acc_ref[h, rows, :] = (alpha * acc_ref[h, rows, :]
                               + _dot(p.astype(BF16), v_ref[pl.ds(key0, nk), vc:vc + LANES]))

    every = slice(0, tq)
    lower = slice(half, tq)

    def body2(kb2, carry):
        r0 = pl.multiple_of(kb2 * (2 * tk), 2 * tk)
        for h in range(GROUP_HEADS):
            s = jnp.concatenate([scores(h, every, r0, tk, 2 * kb2, slice(0, tk)),
                                 scores(h, every, r0 + tk, tk, 2 * kb2 + 1, slice(0, tk))], axis=1)
            update(h, every, s, r0)
        return carry

    lax.fori_loop(0, i // 2, body2, 0)

    @pl.when(i % 2 == 1)
    def _():
        r0 = pl.multiple_of((i - 1) * tk, tk)
        for h in range(GROUP_HEADS):
            update(h, every, scores(h, every, r0, tk, i - 1, slice(0, tk)), r0)

    r0 = pl.multiple_of(i * tk, tk)
    r1 = pl.multiple_of(i * tk + half, half)
    causal = (lax.broadcasted_iota(jnp.int32, (half, half), 1)
              <= lax.broadcasted_iota(jnp.int32, (half, half), 0))
    for h in range(GROUP_HEADS):
        s = scores(h, every, r0, half, i, slice(0, half))
        s = jnp.concatenate([jnp.where(causal, s[:half], NEG_INF), s[half:]], axis=0)
        update(h, every, s, r0)
        s = jnp.where(causal, scores(h, lower, r1, half, i, slice(half, tk)), NEG_INF)
        update(h, lower, s, r1)
    heads = [acc_ref[h] / l_ref[h] for h in range(GROUP_HEADS)]
    _pair_norm_store(o_ref, heads, g_ref)


def _softmax_attn(q, k, v, cum, g, *, B, S, k_cols, name):
    M = q.shape[0]
    nq = S // TQ
    kw = k.shape[1]
    has_bias = cum is not None
    in_specs = [
        pl.BlockSpec((TQ, 4 * LANES), lambda b, i: (b * nq + i, 0)),
        pl.BlockSpec((S, kw), lambda b, i: (b, 0)),
        pl.BlockSpec((S, GROUP_WIDTH), lambda b, i: (b, 0)),
    ]
    args = [q, k, v]
    if has_bias:
        in_specs.append(pl.BlockSpec((1, nq, 8, TQ), lambda b, i: (b, 0, 0, 0)))
        args.append(cum)
    in_specs.append(pl.BlockSpec((1, GROUP_WIDTH), lambda b, i: (0, 0)))
    args.append(g)
    kern = functools.partial(_softmax_attn_kernel, k_cols=k_cols, has_bias=has_bias)
    return pl.pallas_call(
        kern, out_shape=jax.ShapeDtypeStruct((M, GROUP_WIDTH), BF16), grid=(B, nq),
        in_specs=in_specs,
        out_specs=pl.BlockSpec((TQ, GROUP_WIDTH), lambda b, i: (b * nq + i, 0)),
        scratch_shapes=[pltpu.VMEM((GROUP_HEADS, TQ, LANES), F32)] * 3,
        compiler_params=pltpu.CompilerParams(dimension_semantics=("arbitrary", "arbitrary"),
                                             vmem_limit_bytes=VMEM_LIMIT),
        name=name,
    )(*args)


def _stick_kernel(q_ref, k_ref, v_ref, tri_ref, g_ref, o_ref, rest_ref, acc_ref):
    tq = q_ref.shape[0]
    tk = tq
    tw = tri_ref.shape[0]
    i = pl.program_id(1)
    rest_ref[...] = jnp.zeros(rest_ref.shape, F32)
    acc_ref[...] = jnp.zeros(acc_ref.shape, F32)

    half = tq // 2
    assert half % tw == 0
    every = slice(0, tq)
    lower = slice(half, tq)
    strict = (lax.broadcasted_iota(jnp.int32, (half, half), 1)
              < lax.broadcasted_iota(jnp.int32, (half, half), 0))

    def mask_rows(x, rows_masked, fill):
        if rows_masked == 0:
            return x
        top = jnp.where(strict, x[:half], fill)
        return top if rows_masked == x.shape[0] else jnp.concatenate([top, x[half:]], axis=0)

    def block(h, rows, key0, nk, rows_masked):
        c0 = LANES * (h // 2)
        tri = tri_ref[...]
        z = _nt_dot(q_ref[rows, LANES * h:LANES * (h + 1)], k_ref[pl.ds(key0, nk), c0:c0 + LANES])
        z_neg = jnp.minimum(z, 0.0)
        lg = jnp.log(1.0 + jnp.exp2(z_neg - jnp.maximum(z, 0.0))) * LOG2E
        log_b = z_neg - lg
        log_1mb = mask_rows(log_b - z, rows_masked, 0.0)
        rest = rest_ref[h, rows, :]
        parts = [None] * (nk // tw)
        for c in reversed(range(nk // tw)):
            sub = log_1mb[:, tw * c:tw * (c + 1)]
            parts[c] = (_dot(sub.astype(BF16), tri)
                        + jnp.concatenate([rest] * (tw // LANES), axis=1))
            rest = rest + jnp.sum(sub, axis=-1, keepdims=True)
        rest_ref[h, rows, :] = rest
        a = mask_rows(jnp.exp2(log_b + jnp.concatenate(parts, axis=1)), rows_masked, 0.0)
        acc_ref[h, rows, :] = (acc_ref[h, rows, :]
                               + _dot(a.astype(BF16), v_ref[pl.ds(key0, nk), c0:c0 + LANES]))

    r0 = pl.multiple_of(i * tk, tk)
    r1 = pl.multiple_of(i * tk + half, half)
    for h in range(GROUP_HEADS):
        block(h, lower, r1, half, half)
        block(h, every, r0, half, half)

    def body2(t, carry):
        k0 = pl.multiple_of((i - 2 - 2 * t) * tk, tk)
        for h in range(GROUP_HEADS):
            block(h, every, k0, 2 * tk, 0)
        return carry

    lax.fori_loop(0, i // 2, body2, 0)

    @pl.when(i % 2 == 1)
    def _():
        for h in range(GROUP_HEADS):
            block(h, every, 0, tk, 0)
    _pair_norm_store(o_ref, [acc_ref[h] for h in range(GROUP_HEADS)], g_ref)


def _stick_attn(q, k, v, tri, g, *, B, S):
    M = q.shape[0]
    nq = S // TQ
    return pl.pallas_call(
        _stick_kernel, out_shape=jax.ShapeDtypeStruct((M, GROUP_WIDTH), BF16), grid=(B, nq),
        in_specs=[
            pl.BlockSpec((TQ, 4 * LANES), lambda b, i: (b * nq + i, 0)),
            pl.BlockSpec((S, GROUP_WIDTH), lambda b, i: (b, 0)),
            pl.BlockSpec((S, GROUP_WIDTH), lambda b, i: (b, 0)),
            pl.BlockSpec((TRI_W, TRI_W), lambda b, i: (0, 0)),
            pl.BlockSpec((1, GROUP_WIDTH), lambda b, i: (0, 0)),
        ],
        out_specs=pl.BlockSpec((TQ, GROUP_WIDTH), lambda b, i: (b * nq + i, 0)),
        scratch_shapes=[pltpu.VMEM((GROUP_HEADS, TQ, LANES), F32)] * 2,
        compiler_params=pltpu.CompilerParams(dimension_semantics=("arbitrary", "arbitrary"),
                                             vmem_limit_bytes=VMEM_LIMIT),
        name="stick_attn",
    )(q, k, v, tri, g)


def _swa_kernel(sink_ref, q_ref, kc_ref, kp_ref, vc_ref, vp_ref, g_ref, o_ref, *, slopes):
    tq = q_ref.shape[0]
    w = SWA_WINDOW
    i = pl.program_id(1)
    row = lax.broadcasted_iota(jnp.int32, (w, 2 * w), 0)
    col = lax.broadcasted_iota(jnp.int32, (w, 2 * w), 1)
    dist = row + w - col
    dist_f = dist.astype(F32)
    far = dist >= w
    ahead = dist < 0
    is_prev = col < w
    for r in range(tq // w):
        rows = slice(w * r, w * (r + 1))
        if r == 0:
            k2 = jnp.concatenate([kp_ref[...], kc_ref[0:w, :]], axis=0)
            v2 = jnp.concatenate([vp_ref[...], vc_ref[0:w, :]], axis=0)
        else:
            k2 = kc_ref[w * (r - 1):w * (r + 1), :]
            v2 = vc_ref[w * (r - 1):w * (r + 1), :]
        q4 = jnp.concatenate([q_ref[rows, LANES * s:LANES * (s + 1)] for s in range(GROUP_HEADS)], axis=0)
        s_all = _nt_dot(q4, k2)
        probs, dens = [], []
        for slot in range(GROUP_HEADS):
            head = slot // 2 + 2 * (slot % 2)
            sink = sink_ref[head] * LOG2E
            s = s_all[w * slot:w * (slot + 1), :] - (slopes[head] * LOG2E) * dist_f
            s = jnp.where(far, NEG_INF, jnp.where(ahead, NEG_INF, s))
            if r == 0:
                s = jnp.where(is_prev, jnp.where(i > 0, s, NEG_INF), s)
            m = jnp.maximum(jnp.max(s, axis=-1, keepdims=True), sink)
            p = jnp.exp2(s - m)
            dens.append(jnp.sum(p, axis=-1, keepdims=True) + jnp.exp2(sink - m))
            probs.append(p.astype(BF16))
        pv = _dot(jnp.concatenate(probs, axis=0), v2)
        heads = [pv[w * s:w * (s + 1), :] / dens[s] for s in range(GROUP_HEADS)]
        _pair_norm_store(o_ref, heads, g_ref, rows=rows)


def _swa_attn(sinks, q, k, v, g, *, B, S):
    M = q.shape[0]
    tq = TQ_SWA
    nq = S // tq
    per = tq // SWA_WINDOW
    slopes = tuple(float(2.0 ** (-8.0 * (h + 1) / GROUP_HEADS)) for h in range(GROUP_HEADS))
    cur = lambda b, i: (b * nq + i, 0)
    prev = lambda b, i: (b * nq * per + jnp.maximum(i * per - 1, 0), 0)
    kern = functools.partial(_swa_kernel, slopes=slopes)
    return pl.pallas_call(
        kern, out_shape=jax.ShapeDtypeStruct((M, GROUP_WIDTH), BF16), grid=(B, nq),
        in_specs=[
            pl.BlockSpec(memory_space=pltpu.SMEM),
            pl.BlockSpec((tq, 4 * LANES), cur),
            pl.BlockSpec((tq, LANES), cur),
            pl.BlockSpec((SWA_WINDOW, LANES), prev),
            pl.BlockSpec((tq, LANES), cur),
            pl.BlockSpec((SWA_WINDOW, LANES), prev),
            pl.BlockSpec((1, GROUP_WIDTH), lambda b, i: (0, 0)),
        ],
        out_specs=pl.BlockSpec((tq, GROUP_WIDTH), cur),
        compiler_params=pltpu.CompilerParams(dimension_semantics=("arbitrary", "arbitrary"),
                                             vmem_limit_bytes=VMEM_LIMIT),
        name="swa_attn",
    )(sinks, q, k, k, v, v, g)


def _layernorm(r, g, b):
    mu = jnp.mean(r, axis=-1, keepdims=True)
    d = r - mu
    var = jnp.mean(d * d, axis=-1, keepdims=True)
    return d * lax.rsqrt(var + 1e-5) * g + b


def _out_ffn_kernel(x_ref, ma_ref, mb_ref, mc_ref, md_ref, wo_ref, g1_ref, b1_ref,
                    wg_ref, wu_ref, wd_ref, g2_ref, b2_ref, o_ref, *, alpha):
    y = None
    for n, m_ref in enumerate((ma_ref, mb_ref, mc_ref, md_ref)):
        part = _dot(m_ref[...], wo_ref[GROUP_WIDTH * n:GROUP_WIDTH * (n + 1), :])
        y = part if y is None else y + part
    x1 = _layernorm(alpha * x_ref[...] + y, g1_ref[...], b1_ref[...])
    x1b = x1.astype(BF16)
    f = None
    for c0 in range(0, D_FF, F_CHUNK):
        cols = slice(c0, min(c0 + F_CHUNK, D_FF))
        gate = _dot(x1b, wg_ref[:, cols])
        up = _dot(x1b, wu_ref[:, cols])
        hid = (gate * (1.0 / (1.0 + jnp.exp(-gate))) * up).astype(BF16)
        part = _dot(hid, wd_ref[cols, :])
        f = part if f is None else f + part
    o_ref[...] = _layernorm(alpha * x1 + f, g2_ref[...], b2_ref[...])


def _out_ffn(x2, mixes, wo, g1, b1, wg, wu, wd, g2, b2, *, alpha, layer):
    M = x2.shape[0]
    tm = TM_OUT
    row = lambda i: (i, 0)
    const = lambda i: (0, 0)
    resident = lambda w: pl.BlockSpec((None,) + w.shape[1:], lambda i: (layer, 0, 0),
                                      pipeline_mode=pl.Buffered(1))
    vec = pl.BlockSpec((1, D_MODEL), const)
    in_specs = [pl.BlockSpec((tm, D_MODEL), row)]
    in_specs += [pl.BlockSpec((tm, GROUP_WIDTH), row)] * 4
    in_specs += [resident(wo), vec, vec, resident(wg), resident(wu), resident(wd), vec, vec]
    kern = functools.partial(_out_ffn_kernel, alpha=alpha)
    return pl.pallas_call(
        kern, out_shape=jax.ShapeDtypeStruct((M, D_MODEL), F32), grid=(M // tm,),
        in_specs=in_specs, out_specs=pl.BlockSpec((tm, D_MODEL), row),
        compiler_params=pltpu.CompilerParams(dimension_semantics=("arbitrary",),
                                             vmem_limit_bytes=VMEM_LIMIT),
        name="out_ffn",
    )(x2, *mixes, wo, g1, b1, wg, wu, wd, g2, b2)


def kernel(x, w_in, fox_b_f, mla_g_q, mla_g_kv, mla_w_uq, mla_w_ukv, swa_sinks, mix_g, w_o,
           ln1_g, ln1_b, w_gate, w_up, w_down, ln2_g, ln2_b):
    B, S, D = x.shape
    depth = w_in.shape[0]
    assert D == D_MODEL and S % TM_IN == 0 and S % TQ_SWA == 0 and (B * S) % TM_OUT == 0

    w_in_p = _gather_cols(w_in.astype(BF16), _in_proj_columns())
    wq_p = _gather_cols(mla_w_uq.astype(BF16), _mla_q_columns())
    wkv_p = _gather_cols(mla_w_ukv.astype(BF16), _mla_kv_columns())
    order = _mix_row_order()
    wo_p = _gather_cols(w_o.astype(BF16), order, axis=1)
    mix_g_p = _gather_cols(mix_g, order)
    wg_b, wu_b, wd_b = w_gate.astype(BF16), w_up.astype(BF16), w_down.astype(BF16)
    b_f = jnp.pad(fox_b_f.astype(F32), ((0, 0), (0, LANES - fox_b_f.shape[1])))
    tc, ts = _rope_slot_tables(S)
    idx_in = np.arange(TM_IN)
    tri_incl = jnp.asarray(idx_in[:, None] <= idx_in[None, :], BF16)
    idx_q = np.arange(TRI_W)
    tri_after = jnp.asarray(idx_q[:, None] > idx_q[None, :], BF16)

    k_cols_packed = tuple(LANES * (h // 2) for h in range(GROUP_HEADS))
    k_cols_slots = tuple(LANES * h for h in range(GROUP_HEADS))

    x2 = x.reshape(B * S, D)
    for l in range(depth):
        (qa, ka, va, cum, qb, kb, vb, qc, kc, vc, qd, kd, vd) = _in_proj(
            x2, w_in_p, wq_p, wkv_p, mla_g_q[l][None, :], mla_g_kv[l][None, :],
            b_f[l][None, :], tc, ts, tri_incl, B=B, S=S, layer=l)
        g = mix_g_p[l][None, :]
        mix_a = _softmax_attn(qa, ka, va, cum, g[:, 0:256], B=B, S=S, k_cols=k_cols_packed, name="fox_attn")
        mix_b = _softmax_attn(qb, kb, vb, None, g[:, 256:512], B=B, S=S, k_cols=k_cols_slots, name="mla_attn")
        mix_c = _stick_attn(qc, kc, vc, tri_after, g[:, 512:768], B=B, S=S)
        mix_d = _swa_attn(swa_sinks[l].astype(F32), qd, kd, vd, g[:, 768:1024], B=B, S=S)
        x2 = _out_ffn(x2, (mix_a, mix_b, mix_c, mix_d), wo_p, ln1_g[l][None, :], ln1_b[l][None, :],
                      wg_b, wu_b, wd_b, ln2_g[l][None, :], ln2_b[l][None, :], alpha=ALPHA, layer=l)
    return x2.reshape(B, S, D)
```

```python
import functools

import numpy as np
import jax
import jax.numpy as jnp
from jax import lax
from jax.experimental import pallas as pl
from jax.experimental.pallas import tpu as pltpu

F32 = jnp.float32
BF16 = jnp.bfloat16

D_MODEL = 1024
HEAD_DIM = 64
GROUP_HEADS = 4
GROUP_WIDTH = GROUP_HEADS * HEAD_DIM
MLA_Q_RANK = 256
MLA_KV_RANK = 128
MLA_ROPE = 32
SWA_WINDOW = 128
D_FF = 2816
NEG_INF = -1e30
DEPTH = 4
ALPHA = (2.0 * DEPTH) ** 0.25
ROPE_THETA = 10000.0

LANES = 128
VMEM_LIMIT = 52 * 1024 * 1024

C_FQ, C_FK, C_FV = 0, 256, 512
C_CQ = 768
C_SQ, C_SK, C_SV = 1024, 1280, 1536
C_WQ, C_WK, C_WV = 1792, 2048, 2176
C_KR1, C_KR2 = 2304, 2432
C_CKV, C_GATE = 2560, 2688
IN_PERM_WIDTH = 2816
LOG2E = 1.4426950408889634

TM_IN = 1024
TM_OUT = 512
TQ = 512
TRI_W = 256
TQ_SWA = 512
F_CHUNK = 512


def _nt_dot(a, b):
    return lax.dot_general(a, b, (((1,), (1,)), ((), ())), preferred_element_type=F32)


def _dot(a, b):
    return jnp.dot(a, b, preferred_element_type=F32)


def _log_sigmoid(x):
    return jnp.minimum(x, 0.0) - jnp.log(1.0 + jnp.exp(-jnp.abs(x)))


def _in_proj_columns():
    sizes = (256, 256, 256, 4, MLA_Q_RANK, MLA_KV_RANK, MLA_ROPE, 256, 256, 256, 256, 128, 128)
    starts = np.concatenate([[0], np.cumsum(sizes)[:-1]])
    (fq, fk, fv, fg, cq, ckv, kr, sq, sk, sv, wq, wk, wv) = [int(s) for s in starts]
    idx = np.full((IN_PERM_WIDTH,), -1, np.int64)

    def put(dst, src, n):
        idx[dst:dst + n] = np.arange(src, src + n)

    put(C_FQ, fq, 256); put(C_FK, fk, 256); put(C_FV, fv, 256)
    put(C_CQ, cq, MLA_Q_RANK); put(C_CKV, ckv, MLA_KV_RANK)
    put(C_SQ, sq, 256); put(C_SK, sk, 256); put(C_SV, sv, 256)
    for slot, head in enumerate((0, 2, 1, 3)):
        put(C_WQ + 64 * slot, wq + 64 * head, 64)
    put(C_WK, wk, 128); put(C_WV, wv, 128)
    half = MLA_ROPE // 2
    put(C_KR1 + 64, kr, MLA_ROPE)
    put(C_KR2 + 64, kr + half, half)
    put(C_KR2 + 64 + half, kr, half)
    put(C_GATE, fg, 4)
    return idx


def _gather_cols(w, idx, axis=-1):
    axis = axis % w.ndim
    pieces, start = [], 0
    n = len(idx)
    while start < n:
        stop = start + 1
        if idx[start] < 0:
            while stop < n and idx[stop] < 0:
                stop += 1
            shape = w.shape[:axis] + (stop - start,) + w.shape[axis + 1:]
            pieces.append(jnp.zeros(shape, w.dtype))
        else:
            while stop < n and idx[stop] == idx[stop - 1] + 1:
                stop += 1
            pieces.append(lax.slice_in_dim(w, int(idx[start]), int(idx[stop - 1]) + 1, axis=axis))
        start = stop
    return jnp.concatenate(pieces, axis=axis)


def _mla_q_columns():
    half = MLA_ROPE // 2
    per = HEAD_DIM + MLA_ROPE
    idx = np.full((2 * GROUP_HEADS * LANES,), -1, np.int64)
    for h in range(GROUP_HEADS):
        idx[LANES * h:LANES * h + per] = np.arange(per * h, per * h + per)
        base = GROUP_HEADS * LANES + LANES * h + HEAD_DIM
        idx[base:base + half] = np.arange(per * h + HEAD_DIM + half, per * h + per)
        idx[base + half:base + 2 * half] = np.arange(per * h + HEAD_DIM, per * h + HEAD_DIM + half)
    return idx


def _mla_kv_columns():
    idx = np.full((GROUP_HEADS * LANES + GROUP_WIDTH,), -1, np.int64)
    for h in range(GROUP_HEADS):
        idx[LANES * h:LANES * h + HEAD_DIM] = np.arange(2 * HEAD_DIM * h, 2 * HEAD_DIM * h + HEAD_DIM)
        v0 = GROUP_HEADS * LANES + HEAD_DIM * h
        idx[v0:v0 + HEAD_DIM] = np.arange(2 * HEAD_DIM * h + HEAD_DIM, 2 * HEAD_DIM * (h + 1))
    return idx


def _mix_row_order():
    idx = np.arange(4 * GROUP_WIDTH)
    base = 3 * GROUP_WIDTH
    for slot, head in enumerate((0, 2, 1, 3)):
        idx[base + 64 * slot:base + 64 * (slot + 1)] = np.arange(base + 64 * head, base + 64 * (head + 1))
    return idx


def _rope_slot_tables(S):
    pos = jnp.arange(S, dtype=F32)
    inv = ROPE_THETA ** (-jnp.arange(0, MLA_ROPE, 2, dtype=F32) / MLA_ROPE)
    ang = pos[:, None] * inv[None, :]
    cos, sin = jnp.cos(ang), jnp.sin(ang)
    ones = jnp.ones((S, HEAD_DIM), F32)
    zeros32 = jnp.zeros((S, LANES - HEAD_DIM - MLA_ROPE), F32)
    tc = jnp.concatenate([ones, cos, cos, zeros32], axis=1)
    ts = jnp.concatenate([jnp.zeros((S, HEAD_DIM), F32), -sin, sin, zeros32], axis=1)
    return tc, ts


def _in_proj_kernel(x_ref, w_ref, wq_ref, wkv_ref, gq_ref, gkv_ref, bf_ref, tc_ref, ts_ref, tri_ref,
                    qa_ref, ka_ref, va_ref, cum_ref, qb_ref, kb_ref, vb_ref,
                    qc_ref, kc_ref, vc_ref, qd_ref, kd_ref, vd_ref, carry_ref,
                    *, tiles_per_seq, mla_scale):
    tm = x_ref.shape[0]
    i = pl.program_id(0)
    xb = x_ref[...].astype(BF16)

    def proj(c0, n):
        return _dot(xb, w_ref[:, c0:c0 + n])

    lo_half = lax.broadcasted_iota(jnp.int32, (tm, LANES), 1) < HEAD_DIM

    def store_slots(ref, packed, scale):
        for p in range(2):
            pair = packed[:, LANES * p:LANES * (p + 1)] * scale
            ref[:, 2 * LANES * p:2 * LANES * p + LANES] = jnp.where(lo_half, pair, 0.0).astype(BF16)
            ref[:, 2 * LANES * p + LANES:2 * LANES * (p + 1)] = jnp.where(lo_half, 0.0, pair).astype(BF16)

    head_scale = HEAD_DIM ** -0.5 * LOG2E
    store_slots(qa_ref, proj(C_FQ, 256), head_scale)
    ka_ref[...] = proj(C_FK, 256).astype(BF16)
    va_ref[...] = proj(C_FV, 256).astype(BF16)
    store_slots(qc_ref, proj(C_SQ, 256), head_scale)
    kc_ref[...] = proj(C_SK, 256).astype(BF16)
    vc_ref[...] = proj(C_SV, 256).astype(BF16)
    store_slots(qd_ref, proj(C_WQ, 256), head_scale)
    kv_d = proj(C_WK, 2 * LANES)
    kd_ref[...] = kv_d[:, :LANES].astype(BF16)
    vd_ref[...] = kv_d[:, LANES:].astype(BF16)

    tc = tc_ref[...]
    ts = ts_ref[...]
    cq = proj(C_CQ, MLA_Q_RANK)
    cqn = (cq * lax.rsqrt(jnp.mean(cq * cq, axis=-1, keepdims=True) + 1e-6) * gq_ref[...]).astype(BF16)
    q12 = _dot(cqn, wq_ref[...])
    for h in range(GROUP_HEADS):
        q1 = q12[:, LANES * h:LANES * (h + 1)]
        q2 = q12[:, LANES * (GROUP_HEADS + h):LANES * (GROUP_HEADS + h + 1)]
        qb_ref[:, LANES * h:LANES * (h + 1)] = ((q1 * tc + q2 * ts) * mla_scale).astype(BF16)
    ckv_gate = proj(C_CKV, 2 * LANES)
    ckv = ckv_gate[:, :MLA_KV_RANK]
    ckvn = (ckv * lax.rsqrt(jnp.mean(ckv * ckv, axis=-1, keepdims=True) + 1e-6) * gkv_ref[...]).astype(BF16)
    kv = _dot(ckvn, wkv_ref[...])
    kr12 = proj(C_KR1, 2 * LANES)
    k_rope = kr12[:, :LANES] * tc + kr12[:, LANES:] * ts
    for h in range(GROUP_HEADS):
        kb_ref[:, LANES * h:LANES * (h + 1)] = (kv[:, LANES * h:LANES * (h + 1)] + k_rope).astype(BF16)
    vb_ref[...] = kv[:, GROUP_HEADS * LANES:].astype(BF16)

    log_f = _log_sigmoid(ckv_gate[:, LANES:] + bf_ref[...]) * LOG2E
    lf = jnp.transpose(log_f)[0:8, :]
    hi = lf.astype(BF16)
    lo = (lf - hi.astype(F32)).astype(BF16)
    tri = tri_ref[...]
    local = _dot(hi, tri) + _dot(lo, tri)
    carry = jnp.where(i % tiles_per_seq == 0, 0.0, carry_ref[:, 0:1])
    cum = local + carry
    carry_ref[...] = jnp.broadcast_to(cum[:, tm - 1:tm], carry_ref.shape)
    tk = cum_ref.shape[-1]
    for r in range(tm // tk):
        cum_ref[0, r] = cum[:, tk * r:tk * (r + 1)]


def _in_proj(x2, w_in_p, wq_p, wkv_p, g_q, g_kv, b_f, tc, ts, tri, *, B, S, layer):
    M = x2.shape[0]
    tm = TM_IN
    tiles_per_seq = S // tm
    row = lambda i: (i, 0)
    const = lambda i: (0, 0)
    stacked = lambda w: pl.BlockSpec((None,) + w.shape[1:], lambda i: (layer, 0, 0))
    pos = lambda i: (i % tiles_per_seq, 0)
    bf = lambda w: jax.ShapeDtypeStruct((M, w), BF16)
    out_shape = (bf(512), bf(256), bf(256),
                 jax.ShapeDtypeStruct((B, S // TQ, 8, TQ), F32),
                 bf(512), bf(512), bf(256),
                 bf(512), bf(256), bf(256),
                 bf(512), bf(128), bf(128))
    ob = lambda w: pl.BlockSpec((tm, w), row)
    out_specs = (ob(512), ob(256), ob(256),
                 pl.BlockSpec((1, tm // TQ, 8, TQ), lambda i: (i // tiles_per_seq, i % tiles_per_seq, 0, 0)),
                 ob(512), ob(512), ob(256),
                 ob(512), ob(256), ob(256),
                 ob(512), ob(128), ob(128))
    in_specs = [
        pl.BlockSpec((tm, D_MODEL), row),
        stacked(w_in_p),
        stacked(wq_p),
        stacked(wkv_p),
        pl.BlockSpec((1, MLA_Q_RANK), const),
        pl.BlockSpec((1, MLA_KV_RANK), const),
        pl.BlockSpec((1, LANES), const),
        pl.BlockSpec((tm, LANES), pos),
        pl.BlockSpec((tm, LANES), pos),
        pl.BlockSpec(tri.shape, const),
    ]
    kern = functools.partial(_in_proj_kernel, tiles_per_seq=tiles_per_seq,
                             mla_scale=float((HEAD_DIM + MLA_ROPE) ** -0.5 * LOG2E))
    return pl.pallas_call(
        kern, out_shape=out_shape, grid=(M // tm,), in_specs=in_specs, out_specs=out_specs,
        scratch_shapes=[pltpu.VMEM((8, LANES), F32)],
        compiler_params=pltpu.CompilerParams(dimension_semantics=("arbitrary",),
                                             vmem_limit_bytes=VMEM_LIMIT),
        name="in_proj",
    )(x2, w_in_p, wq_p, wkv_p, g_q, g_kv, b_f, tc, ts, tri)


def _pair_norm_store(o_ref, heads, g_ref, rows=None):
    t = heads[0].shape[0]
    lo_half = lax.broadcasted_iota(jnp.int32, (t, LANES), 1) < HEAD_DIM
    pairs = [jnp.where(lo_half, heads[2 * p], heads[2 * p + 1]) for p in range(2)]
    ss = sum(jnp.sum(p * p, axis=-1, keepdims=True) for p in pairs)
    inv = lax.rsqrt(ss * (1.0 / GROUP_WIDTH) + 1e-6)
    for p in range(2):
        val = (pairs[p] * inv * g_ref[:, LANES * p:LANES * (p + 1)]).astype(o_ref.dtype)
        if rows is None:
            o_ref[:, LANES * p:LANES * (p + 1)] = val
        else:
            o_ref[rows, LANES * p:LANES * (p + 1)] = val


def _softmax_attn_kernel(*refs, k_cols, has_bias):
    if has_bias:
        q_ref, k_ref, v_ref, cum_ref, g_ref, o_ref, m_ref, l_ref, acc_ref = refs
    else:
        q_ref, k_ref, v_ref, g_ref, o_ref, m_ref, l_ref, acc_ref = refs
        cum_ref = None
    tq = q_ref.shape[0]
    tk = tq
    i = pl.program_id(1)
    m_ref[...] = jnp.full(m_ref.shape, NEG_INF, F32)
    l_ref[...] = jnp.zeros(l_ref.shape, F32)
    acc_ref[...] = jnp.zeros(acc_ref.shape, F32)

    half = tq // 2

    def scores(h, rows, key0, nk, kb, cum_cols):
        kc = k_cols[h]
        s = _nt_dot(q_ref[rows, LANES * h:LANES * (h + 1)], k_ref[pl.ds(key0, nk), kc:kc + LANES])
        if has_bias:
            s = s - cum_ref[0, kb][h:h + 1, cum_cols]
        return s

    def update(h, rows, s, key0):
        nk = s.shape[1]
        vc = LANES * (h // 2)
        m_prev = m_ref[h, rows, :]
        m_next = jnp.maximum(m_prev, jnp.max(s, axis=-1, keepdims=True))
        alpha = jnp.exp2(m_prev - m_next)
        p = jnp.exp2(s - jnp.concatenate([m_next] * (nk // LANES), axis=1))
        l_ref[h, rows, :] = alpha * l_ref[h, rows, :] + jnp.sum(p, axis=-1, keepdims=True)
        m_ref[h, rows, :] = m_next
        acc_ref[h, rows, :] = (alpha * acc_ref[h, rows, :]
                               + _dot(p.astype(BF16), v_ref[pl.ds(key0, nk), vc:vc + LANES]))

    every = slice(0, tq)
    lower = slice(half, tq)

    def body2(kb2, carry):
        r0 = pl.multiple_of(kb2 * (2 * tk), 2 * tk)
        for h in range(GROUP_HEADS):
            s = jnp.concatenate([scores(h, every, r0, tk, 2 * kb2, slice(0, tk)),
                                 scores(h, every, r0 + tk, tk, 2 * kb2 + 1, slice(0, tk))], axis=1)
            update(h, every, s, r0)
        return carry

    lax.fori_loop(0, i // 2, body2, 0)

    @pl.when(i % 2 == 1)
    def _():
        r0 = pl.multiple_of((i - 1) * tk, tk)
        for h in range(GROUP_HEADS):
            update(h, every, scores(h, every, r0, tk, i - 1, slice(0, tk)), r0)

    r0 = pl.multiple_of(i * tk, tk)
    r1 = pl.multiple_of(i * tk + half, half)
    causal = (lax.broadcasted_iota(jnp.int32, (half, half), 1)
              <= lax.broadcasted_iota(jnp.int32, (half, half), 0))
    for h in range(GROUP_HEADS):
        s = scores(h, every, r0, half, i, slice(0, half))
        s = jnp.concatenate([jnp.where(causal, s[:half], NEG_INF), s[half:]], axis=0)
        update(h, every, s, r0)
        s = jnp.where(causal, scores(h, lower, r1, half, i, slice(half, tk)), NEG_INF)
        update(h, lower, s, r1)
    heads = [acc_ref[h] / l_ref[h] for h in range(GROUP_HEADS)]
    _pair_norm_store(o_ref, heads, g_ref)


def _softmax_attn(q, k, v, cum, g, *, B, S, k_cols, name):
    M = q.shape[0]
    nq = S // TQ
    kw = k.shape[1]
    has_bias = cum is not None
    in_specs = [
        pl.BlockSpec((TQ, 4 * LANES), lambda b, i: (b * nq + i, 0)),
        pl.BlockSpec((S, kw), lambda b, i: (b, 0)),
        pl.BlockSpec((S, GROUP_WIDTH), lambda b, i: (b, 0)),
    ]
    args = [q, k, v]
    if has_bias:
        in_specs.append(pl.BlockSpec((1, nq, 8, TQ), lambda b, i: (b, 0, 0, 0)))
        args.append(cum)
    in_specs.append(pl.BlockSpec((1, GROUP_WIDTH), lambda b, i: (0, 0)))
    args.append(g)
    kern = functools.partial(_softmax_attn_kernel, k_cols=k_cols, has_bias=has_bias)
    return pl.pallas_call(
        kern, out_shape=jax.ShapeDtypeStruct((M, GROUP_WIDTH), BF16), grid=(B, nq),
        in_specs=in_specs,
        out_specs=pl.BlockSpec((TQ, GROUP_WIDTH), lambda b, i: (b * nq + i, 0)),
        scratch_shapes=[pltpu.VMEM((GROUP_HEADS, TQ, LANES), F32)] * 3,
        compiler_params=pltpu.CompilerParams(dimension_semantics=("arbitrary", "arbitrary"),
                                             vmem_limit_bytes=VMEM_LIMIT),
        name=name,
    )(*args)


def _stick_kernel(q_ref, k_ref, v_ref, tri_ref, g_ref, o_ref, rest_ref, acc_ref):
    tq = q_ref.shape[0]
    tk = tq
    tw = tri_ref.shape[0]
    i = pl.program_id(1)
    rest_ref[...] = jnp.zeros(rest_ref.shape, F32)
    acc_ref[...] = jnp.zeros(acc_ref.shape, F32)

    half = tq // 2
    assert half % tw == 0
    every = slice(0, tq)
    lower = slice(half, tq)
    strict = (lax.broadcasted_iota(jnp.int32, (half, half), 1)
              < lax.broadcasted_iota(jnp.int32, (half, half), 0))

    def mask_rows(x, rows_masked, fill):
        if rows_masked == 0:
            return x
        top = jnp.where(strict, x[:half], fill)
        return top if rows_masked == x.shape[0] else jnp.concatenate([top, x[half:]], axis=0)

    def block(h, rows, key0, nk, rows_masked):
        c0 = LANES * (h // 2)
        tri = tri_ref[...]
        z = _nt_dot(q_ref[rows, LANES * h:LANES * (h + 1)], k_ref[pl.ds(key0, nk), c0:c0 + LANES])
        z_neg = jnp.minimum(z, 0.0)
        lg = jnp.log(1.0 + jnp.exp2(-jnp.abs(z))) * LOG2E
        log_b = z_neg - lg
        log_1mb = mask_rows(log_b - z, rows_masked, 0.0)
        rest = rest_ref[h, rows, :]
        parts = [None] * (nk // tw)
        for c in reversed(range(nk // tw)):
            sub = log_1mb[:, tw * c:tw * (c + 1)]
            parts[c] = (_dot(sub.astype(BF16), tri)
                        + jnp.concatenate([rest] * (tw // LANES), axis=1))
            rest = rest + jnp.sum(sub, axis=-1, keepdims=True)
        rest_ref[h, rows, :] = rest
        a = mask_rows(jnp.exp2(log_b + jnp.concatenate(parts, axis=1)), rows_masked, 0.0)
        acc_ref[h, rows, :] = (acc_ref[h, rows, :]
                               + _dot(a.astype(BF16), v_ref[pl.ds(key0, nk), c0:c0 + LANES]))

    r0 = pl.multiple_of(i * tk, tk)
    r1 = pl.multiple_of(i * tk + half, half)
    for h in range(GROUP_HEADS):
        block(h, lower, r1, half, half)
        block(h, every, r0, half, half)

    def body2(t, carry):
        k0 = pl.multiple_of((i - 2 - 2 * t) * tk, tk)
        for h in range(GROUP_HEADS):
            block(h, every, k0, 2 * tk, 0)
        return carry

    lax.fori_loop(0, i // 2, body2, 0)

    @pl.when(i % 2 == 1)
    def _():
        for h in range(GROUP_HEADS):
            block(h, every, 0, tk, 0)
    _pair_norm_store(o_ref, [acc_ref[h] for h in range(GROUP_HEADS)], g_ref)


def _stick_attn(q, k, v, tri, g, *, B, S):
    M = q.shape[0]
    nq = S // TQ
    return pl.pallas_call(
        _stick_kernel, out_shape=jax.ShapeDtypeStruct((M, GROUP_WIDTH), BF16), grid=(B, nq),
        in_specs=[
            pl.BlockSpec((TQ, 4 * LANES), lambda b, i: (b * nq + i, 0)),
            pl.BlockSpec((S, GROUP_WIDTH), lambda b, i: (b, 0)),
            pl.BlockSpec((S, GROUP_WIDTH), lambda b, i: (b, 0)),
            pl.BlockSpec((TRI_W, TRI_W), lambda b, i: (0, 0)),
            pl.BlockSpec((1, GROUP_WIDTH), lambda b, i: (0, 0)),
        ],
        out_specs=pl.BlockSpec((TQ, GROUP_WIDTH), lambda b, i: (b * nq + i, 0)),
        scratch_shapes=[pltpu.VMEM((GROUP_HEADS, TQ, LANES), F32)] * 2,
        compiler_params=pltpu.CompilerParams(dimension_semantics=("arbitrary", "arbitrary"),
                                             vmem_limit_bytes=VMEM_LIMIT),
        name="stick_attn",
    )(q, k, v, tri, g)


def _swa_kernel(sink_ref, q_ref, kc_ref, kp_ref, vc_ref, vp_ref, g_ref, o_ref, *, slopes):
    tq = q_ref.shape[0]
    w = SWA_WINDOW
    i = pl.program_id(1)
    row = lax.broadcasted_iota(jnp.int32, (w, 2 * w), 0)
    col = lax.broadcasted_iota(jnp.int32, (w, 2 * w), 1)
    dist = row + w - col
    dist_f = dist.astype(F32)
    far = dist >= w
    ahead = dist < 0
    is_prev = col < w
    for r in range(tq // w):
        rows = slice(w * r, w * (r + 1))
        if r == 0:
            k2 = jnp.concatenate([kp_ref[...], kc_ref[0:w, :]], axis=0)
            v2 = jnp.concatenate([vp_ref[...], vc_ref[0:w, :]], axis=0)
        else:
            k2 = kc_ref[w * (r - 1):w * (r + 1), :]
            v2 = vc_ref[w * (r - 1):w * (r + 1), :]
        q4 = jnp.concatenate([q_ref[rows, LANES * s:LANES * (s + 1)] for s in range(GROUP_HEADS)], axis=0)
        s_all = _nt_dot(q4, k2)
        probs, dens = [], []
        for slot in range(GROUP_HEADS):
            head = slot // 2 + 2 * (slot % 2)
            sink = sink_ref[head] * LOG2E
            s = s_all[w * slot:w * (slot + 1), :] - (slopes[head] * LOG2E) * dist_f
            s = jnp.where(far, NEG_INF, jnp.where(ahead, NEG_INF, s))
            if r == 0:
                s = jnp.where(is_prev, jnp.where(i > 0, s, NEG_INF), s)
            m = jnp.maximum(jnp.max(s, axis=-1, keepdims=True), sink)
            p = jnp.exp2(s - m)
            dens.append(jnp.sum(p, axis=-1, keepdims=True) + jnp.exp2(sink - m))
            probs.append(p.astype(BF16))
        pv = _dot(jnp.concatenate(probs, axis=0), v2)
        heads = [pv[w * s:w * (s + 1), :] / dens[s] for s in range(GROUP_HEADS)]
        _pair_norm_store(o_ref, heads, g_ref, rows=rows)


def _swa_attn(sinks, q, k, v, g, *, B, S):
    M = q.shape[0]
    tq = TQ_SWA
    nq = S // tq
    per = tq // SWA_WINDOW
    slopes = tuple(float(2.0 ** (-8.0 * (h + 1) / GROUP_HEADS)) for h in range(GROUP_HEADS))
    cur = lambda b, i: (b * nq + i, 0)
    prev = lambda b, i: (b * nq * per + jnp.maximum(i * per - 1, 0), 0)
    kern = functools.partial(_swa_kernel, slopes=slopes)
    return pl.pallas_call(
        kern, out_shape=jax.ShapeDtypeStruct((M, GROUP_WIDTH), BF16), grid=(B, nq),
        in_specs=[
            pl.BlockSpec(memory_space=pltpu.SMEM),
            pl.BlockSpec((tq, 4 * LANES), cur),
            pl.BlockSpec((tq, LANES), cur),
            pl.BlockSpec((SWA_WINDOW, LANES), prev),
            pl.BlockSpec((tq, LANES), cur),
            pl.BlockSpec((SWA_WINDOW, LANES), prev),
            pl.BlockSpec((1, GROUP_WIDTH), lambda b, i: (0, 0)),
        ],
        out_specs=pl.BlockSpec((tq, GROUP_WIDTH), cur),
        compiler_params=pltpu.CompilerParams(dimension_semantics=("arbitrary", "arbitrary"),
                                             vmem_limit_bytes=VMEM_LIMIT),
        name="swa_attn",
    )(sinks, q, k, k, v, v, g)


def _layernorm(r, g, b):
    mu = jnp.mean(r, axis=-1, keepdims=True)
    d = r - mu
    var = jnp.mean(d * d, axis=-1, keepdims=True)
    return d * lax.rsqrt(var + 1e-5) * g + b


def _out_ffn_kernel(x_ref, ma_ref, mb_ref, mc_ref, md_ref, wo_ref, g1_ref, b1_ref,
                    wg_ref, wu_ref, wd_ref, g2_ref, b2_ref, o_ref, *, alpha):
    y = None
    for n, m_ref in enumerate((ma_ref, mb_ref, mc_ref, md_ref)):
        part = _dot(m_ref[...], wo_ref[GROUP_WIDTH * n:GROUP_WIDTH * (n + 1), :])
        y = part if y is None else y + part
    x1 = _layernorm(alpha * x_ref[...] + y, g1_ref[...], b1_ref[...])
    x1b = x1.astype(BF16)
    f = None
    for c0 in range(0, D_FF, F_CHUNK):
        cols = slice(c0, min(c0 + F_CHUNK, D_FF))
        gate = _dot(x1b, wg_ref[:, cols])
        up = _dot(x1b, wu_ref[:, cols])
        hid = (gate * (1.0 / (1.0 + jnp.exp(-gate))) * up).astype(BF16)
        part = _dot(hid, wd_ref[cols, :])
        f = part if f is None else f + part
    o_ref[...] = _layernorm(alpha * x1 + f, g2_ref[...], b2_ref[...])


def _out_ffn(x2, mixes, wo, g1, b1, wg, wu, wd, g2, b2, *, alpha, layer):
    M = x2.shape[0]
    tm = TM_OUT
    row = lambda i: (i, 0)
    const = lambda i: (0, 0)
    resident = lambda w: pl.BlockSpec((None,) + w.shape[1:], lambda i: (layer, 0, 0),
                                      pipeline_mode=pl.Buffered(1))
    vec = pl.BlockSpec((1, D_MODEL), const)
    in_specs = [pl.BlockSpec((tm, D_MODEL), row)]
    in_specs += [pl.BlockSpec((tm, GROUP_WIDTH), row)] * 4
    in_specs += [resident(wo), vec, vec, resident(wg), resident(wu), resident(wd), vec, vec]
    kern = functools.partial(_out_ffn_kernel, alpha=alpha)
    return pl.pallas_call(
        kern, out_shape=jax.ShapeDtypeStruct((M, D_MODEL), F32), grid=(M // tm,),
        in_specs=in_specs, out_specs=pl.BlockSpec((tm, D_MODEL), row),
        compiler_params=pltpu.CompilerParams(dimension_semantics=("arbitrary",),
                                             vmem_limit_bytes=VMEM_LIMIT),
        name="out_ffn",
    )(x2, *mixes, wo, g1, b1, wg, wu, wd, g2, b2)


def kernel(x, w_in, fox_b_f, mla_g_q, mla_g_kv, mla_w_uq, mla_w_ukv, swa_sinks, mix_g, w_o,
           ln1_g, ln1_b, w_gate, w_up, w_down, ln2_g, ln2_b):
    B, S, D = x.shape
    depth = w_in.shape[0]
    assert D == D_MODEL and S % TM_IN == 0 and S % TQ_SWA == 0 and (B * S) % TM_OUT == 0

    w_in_p = _gather_cols(w_in.astype(BF16), _in_proj_columns())
    wq_p = _gather_cols(mla_w_uq.astype(BF16), _mla_q_columns())
    wkv_p = _gather_cols(mla_w_ukv.astype(BF16), _mla_kv_columns())
    order = _mix_row_order()
    wo_p = _gather_cols(w_o.astype(BF16), order, axis=1)
    mix_g_p = _gather_cols(mix_g, order)
    wg_b, wu_b, wd_b = w_gate.astype(BF16), w_up.astype(BF16), w_down.astype(BF16)
    b_f = jnp.pad(fox_b_f.astype(F32), ((0, 0), (0, LANES - fox_b_f.shape[1])))
    tc, ts = _rope_slot_tables(S)
    idx_in = np.arange(TM_IN)
    tri_incl = jnp.asarray(idx_in[:, None] <= idx_in[None, :], BF16)
    idx_q = np.arange(TRI_W)
    tri_after = jnp.asarray(idx_q[:, None] > idx_q[None, :], BF16)

    k_cols_packed = tuple(LANES * (h // 2) for h in range(GROUP_HEADS))
    k_cols_slots = tuple(LANES * h for h in range(GROUP_HEADS))

    x2 = x.reshape(B * S, D)
    for l in range(depth):
        (qa, ka, va, cum, qb, kb, vb, qc, kc, vc, qd, kd, vd) = _in_proj(
            x2, w_in_p, wq_p, wkv_p, mla_g_q[l][None, :], mla_g_kv[l][None, :],
            b_f[l][None, :], tc, ts, tri_incl, B=B, S=S, layer=l)
        g = mix_g_p[l][None, :]
        mix_a = _softmax_attn(qa, ka, va, cum, g[:, 0:256], B=B, S=S, k_cols=k_cols_packed, name="fox_attn")
        mix_b = _softmax_attn(qb, kb, vb, None, g[:, 256:512], B=B, S=S, k_cols=k_cols_slots, name="mla_attn")
        mix_c = _stick_attn(qc, kc, vc, tri_after, g[:, 512:768], B=B, S=S)
        mix_d = _swa_attn(swa_sinks[l].astype(F32), qd, kd, vd, g[:, 768:1024], B=B, S=S)
        x2 = _out_ffn(x2, (mix_a, mix_b, mix_c, mix_d), wo_p, ln1_g[l][None, :], ln1_b[l][None, :],
                      wg_b, wu_b, wd_b, ln2_g[l][None, :], ln2_b[l][None, :], alpha=ALPHA, layer=l)
    return x2.reshape(B, S, D)
```

```python
import functools

import numpy as np
import jax
import jax.numpy as jnp
from jax import lax
from jax.experimental import pallas as pl
from jax.experimental.pallas import tpu as pltpu

F32 = jnp.float32
BF16 = jnp.bfloat16

D_MODEL = 1024
HEAD_DIM = 64
GROUP_HEADS = 4
GROUP_WIDTH = GROUP_HEADS * HEAD_DIM
MLA_Q_RANK = 256
MLA_KV_RANK = 128
MLA_ROPE = 32
SWA_WINDOW = 128
D_FF = 2816
NEG_INF = -1e30
DEPTH = 4
ALPHA = (2.0 * DEPTH) ** 0.25
ROPE_THETA = 10000.0

LANES = 128
VMEM_LIMIT = 52 * 1024 * 1024

C_FQ, C_FK, C_FV = 0, 256, 512
C_CQ = 768
C_SQ, C_SK, C_SV = 1024, 1280, 1536
C_WQ, C_WK, C_WV = 1792, 2048, 2176
C_KR1, C_KR2 = 2304, 2432
C_CKV, C_GATE = 2560, 2688
IN_PERM_WIDTH = 2816
LOG2E = 1.4426950408889634

TM_IN = 1024
TM_OUT = 512
TQ = 512
TRI_W = 256
TQ_SWA = 512
F_CHUNK = 512


def _nt_dot(a, b):
    return lax.dot_general(a, b, (((1,), (1,)), ((), ())), preferred_element_type=F32)


def _dot(a, b):
    return jnp.dot(a, b, preferred_element_type=F32)


def _log_sigmoid(x):
    return jnp.minimum(x, 0.0) - jnp.log(1.0 + jnp.exp(-jnp.abs(x)))


def _in_proj_columns():
    sizes = (256, 256, 256, 4, MLA_Q_RANK, MLA_KV_RANK, MLA_ROPE, 256, 256, 256, 256, 128, 128)
    starts = np.concatenate([[0], np.cumsum(sizes)[:-1]])
    (fq, fk, fv, fg, cq, ckv, kr, sq, sk, sv, wq, wk, wv) = [int(s) for s in starts]
    idx = np.full((IN_PERM_WIDTH,), -1, np.int64)

    def put(dst, src, n):
        idx[dst:dst + n] = np.arange(src, src + n)

    put(C_FQ, fq, 256); put(C_FK, fk, 256); put(C_FV, fv, 256)
    put(C_CQ, cq, MLA_Q_RANK); put(C_CKV, ckv, MLA_KV_RANK)
    put(C_SQ, sq, 256); put(C_SK, sk, 256); put(C_SV, sv, 256)
    for slot, head in enumerate((0, 2, 1, 3)):
        put(C_WQ + 64 * slot, wq + 64 * head, 64)
    put(C_WK, wk, 128); put(C_WV, wv, 128)
    half = MLA_ROPE // 2
    put(C_KR1 + 64, kr, MLA_ROPE)
    put(C_KR2 + 64, kr + half, half)
    put(C_KR2 + 64 + half, kr, half)
    put(C_GATE, fg, 4)
    return idx


def _gather_cols(w, idx, axis=-1):
    axis = axis % w.ndim
    pieces, start = [], 0
    n = len(idx)
    while start < n:
        stop = start + 1
        if idx[start] < 0:
            while stop < n and idx[stop] < 0:
                stop += 1
            shape = w.shape[:axis] + (stop - start,) + w.shape[axis + 1:]
            pieces.append(jnp.zeros(shape, w.dtype))
        else:
            while stop < n and idx[stop] == idx[stop - 1] + 1:
                stop += 1
            pieces.append(lax.slice_in_dim(w, int(idx[start]), int(idx[stop - 1]) + 1, axis=axis))
        start = stop
    return jnp.concatenate(pieces, axis=axis)


def _mla_q_columns():
    half = MLA_ROPE // 2
    per = HEAD_DIM + MLA_ROPE
    idx = np.full((2 * GROUP_HEADS * LANES,), -1, np.int64)
    for h in range(GROUP_HEADS):
        idx[LANES * h:LANES * h + per] = np.arange(per * h, per * h + per)
        base = GROUP_HEADS * LANES + LANES * h + HEAD_DIM
        idx[base:base + half] = np.arange(per * h + HEAD_DIM + half, per * h + per)
        idx[base + half:base + 2 * half] = np.arange(per * h + HEAD_DIM, per * h + HEAD_DIM + half)
    return idx


def _mla_kv_columns():
    idx = np.full((GROUP_HEADS * LANES + GROUP_WIDTH,), -1, np.int64)
    for h in range(GROUP_HEADS):
        idx[LANES * h:LANES * h + HEAD_DIM] = np.arange(2 * HEAD_DIM * h, 2 * HEAD_DIM * h + HEAD_DIM)
        v0 = GROUP_HEADS * LANES + HEAD_DIM * h
        idx[v0:v0 + HEAD_DIM] = np.arange(2 * HEAD_DIM * h + HEAD_DIM, 2 * HEAD_DIM * (h + 1))
    return idx


def _mix_row_order():
    idx = np.arange(4 * GROUP_WIDTH)
    base = 3 * GROUP_WIDTH
    for slot, head in enumerate((0, 2, 1, 3)):
        idx[base + 64 * slot:base + 64 * (slot + 1)] = np.arange(base + 64 * head, base + 64 * (head + 1))
    return idx


def _rope_slot_tables(S):
    pos = jnp.arange(S, dtype=F32)
    inv = ROPE_THETA ** (-jnp.arange(0, MLA_ROPE, 2, dtype=F32) / MLA_ROPE)
    ang = pos[:, None] * inv[None, :]
    cos, sin = jnp.cos(ang), jnp.sin(ang)
    ones = jnp.ones((S, HEAD_DIM), F32)
    zeros32 = jnp.zeros((S, LANES - HEAD_DIM - MLA_ROPE), F32)
    tc = jnp.concatenate([ones, cos, cos, zeros32], axis=1)
    ts = jnp.concatenate([jnp.zeros((S, HEAD_DIM), F32), -sin, sin, zeros32], axis=1)
    return tc, ts


def _in_proj_kernel(x_ref, w_ref, wq_ref, wkv_ref, gq_ref, gkv_ref, bf_ref, tc_ref, ts_ref, tri_ref,
                    qa_ref, ka_ref, va_ref, cum_ref, qb_ref, kb_ref, vb_ref,
                    qc_ref, kc_ref, vc_ref, qd_ref, kd_ref, vd_ref, carry_ref,
                    *, tiles_per_seq, mla_scale):
    tm = x_ref.shape[0]
    i = pl.program_id(0)

    @pl.when(i == 0)
    def _():
        carry_ref[...] = jnp.zeros(carry_ref.shape, F32)

    xb = x_ref[...].astype(BF16)

    def proj(c0, n):
        return _dot(xb, w_ref[:, c0:c0 + n])

    lo_half = lax.broadcasted_iota(jnp.int32, (tm, LANES), 1) < HEAD_DIM

    def store_slots(ref, packed, scale):
        for p in range(2):
            pair = packed[:, LANES * p:LANES * (p + 1)] * scale
            ref[:, 2 * LANES * p:2 * LANES * p + LANES] = jnp.where(lo_half, pair, 0.0).astype(BF16)
            ref[:, 2 * LANES * p + LANES:2 * LANES * (p + 1)] = jnp.where(lo_half, 0.0, pair).astype(BF16)

    head_scale = HEAD_DIM ** -0.5 * LOG2E
    store_slots(qa_ref, proj(C_FQ, 256), head_scale)
    ka_ref[...] = proj(C_FK, 256).astype(BF16)
    va_ref[...] = proj(C_FV, 256).astype(BF16)
    store_slots(qc_ref, proj(C_SQ, 256), head_scale)
    kc_ref[...] = proj(C_SK, 256).astype(BF16)
    vc_ref[...] = proj(C_SV, 256).astype(BF16)
    store_slots(qd_ref, proj(C_WQ, 256), head_scale)
    kv_d = proj(C_WK, 2 * LANES)
    kd_ref[...] = kv_d[:, :LANES].astype(BF16)
    vd_ref[...] = kv_d[:, LANES:].astype(BF16)

    tc = tc_ref[...]
    ts = ts_ref[...]
    cq = proj(C_CQ, MLA_Q_RANK)
    cqn = (cq * lax.rsqrt(jnp.mean(cq * cq, axis=-1, keepdims=True) + 1e-6) * gq_ref[...]).astype(BF16)
    q12 = _dot(cqn, wq_ref[...])
    for h in range(GROUP_HEADS):
        q1 = q12[:, LANES * h:LANES * (h + 1)]
        q2 = q12[:, LANES * (GROUP_HEADS + h):LANES * (GROUP_HEADS + h + 1)]
        qb_ref[:, LANES * h:LANES * (h + 1)] = ((q1 * tc + q2 * ts) * mla_scale).astype(BF16)
    ckv_gate = proj(C_CKV, 2 * LANES)
    ckv = ckv_gate[:, :MLA_KV_RANK]
    ckvn = (ckv * lax.rsqrt(jnp.mean(ckv * ckv, axis=-1, keepdims=True) + 1e-6) * gkv_ref[...]).astype(BF16)
    kv = _dot(ckvn, wkv_ref[...])
    kr12 = proj(C_KR1, 2 * LANES)
    k_rope = kr12[:, :LANES] * tc + kr12[:, LANES:] * ts
    for h in range(GROUP_HEADS):
        kb_ref[:, LANES * h:LANES * (h + 1)] = (kv[:, LANES * h:LANES * (h + 1)] + k_rope).astype(BF16)
    vb_ref[...] = kv[:, GROUP_HEADS * LANES:].astype(BF16)

    log_f = _log_sigmoid(ckv_gate[:, LANES:] + bf_ref[...]) * LOG2E
    lf = jnp.transpose(log_f)[0:8, :]
    hi = lf.astype(BF16)
    lo = (lf - hi.astype(F32)).astype(BF16)
    tri = tri_ref[...]
    local = _dot(hi, tri) + _dot(lo, tri)
    carry = jnp.where(i % tiles_per_seq == 0, 0.0, carry_ref[:, 0:1])
    cum = local + carry
    carry_ref[...] = jnp.broadcast_to(cum[:, tm - 1:tm], carry_ref.shape)
    tk = cum_ref.shape[-1]
    for r in range(tm // tk):
        cum_ref[0, r] = cum[:, tk * r:tk * (r + 1)]


def _in_proj(x2, w_in_p, wq_p, wkv_p, g_q, g_kv, b_f, tc, ts, tri, *, B, S, layer):
    M = x2.shape[0]
    tm = TM_IN
    tiles_per_seq = S // tm
    row = lambda i: (i, 0)
    const = lambda i: (0, 0)
    stacked = lambda w: pl.BlockSpec((None,) + w.shape[1:], lambda i: (layer, 0, 0))
    pos = lambda i: (i % tiles_per_seq, 0)
    bf = lambda w: jax.ShapeDtypeStruct((M, w), BF16)
    out_shape = (bf(512), bf(256), bf(256),
                 jax.ShapeDtypeStruct((B, S // TQ, 8, TQ), F32),
                 bf(512), bf(512), bf(256),
                 bf(512), bf(256), bf(256),
                 bf(512), bf(128), bf(128))
    ob = lambda w: pl.BlockSpec((tm, w), row)
    out_specs = (ob(512), ob(256), ob(256),
                 pl.BlockSpec((1, tm // TQ, 8, TQ), lambda i: (i // tiles_per_seq, i % tiles_per_seq, 0, 0)),
                 ob(512), ob(512), ob(256),
                 ob(512), ob(256), ob(256),
                 ob(512), ob(128), ob(128))
    in_specs = [
        pl.BlockSpec((tm, D_MODEL), row),
        stacked(w_in_p),
        stacked(wq_p),
        stacked(wkv_p),
        pl.BlockSpec((1, MLA_Q_RANK), const),
        pl.BlockSpec((1, MLA_KV_RANK), const),
        pl.BlockSpec((1, LANES), const),
        pl.BlockSpec((tm, LANES), pos),
        pl.BlockSpec((tm, LANES), pos),
        pl.BlockSpec(tri.shape, const),
    ]
    kern = functools.partial(_in_proj_kernel, tiles_per_seq=tiles_per_seq,
                             mla_scale=float((HEAD_DIM + MLA_ROPE) ** -0.5 * LOG2E))
    return pl.pallas_call(
        kern, out_shape=out_shape, grid=(M // tm,), in_specs=in_specs, out_specs=out_specs,
        scratch_shapes=[pltpu.VMEM((8, LANES), F32)],
        compiler_params=pltpu.CompilerParams(dimension_semantics=("arbitrary",),
                                             vmem_limit_bytes=VMEM_LIMIT),
        name="in_proj",
    )(x2, w_in_p, wq_p, wkv_p, g_q, g_kv, b_f, tc, ts, tri)


def _pair_norm_store(o_ref, heads, g_ref, rows=None):
    t = heads[0].shape[0]
    lo_half = lax.broadcasted_iota(jnp.int32, (t, LANES), 1) < HEAD_DIM
    pairs = [jnp.where(lo_half, heads[2 * p], heads[2 * p + 1]) for p in range(2)]
    ss = sum(jnp.sum(p * p, axis=-1, keepdims=True) for p in pairs)
    inv = lax.rsqrt(ss * (1.0 / GROUP_WIDTH) + 1e-6)
    for p in range(2):
        val = (pairs[p] * inv * g_ref[:, LANES * p:LANES * (p + 1)]).astype(o_ref.dtype)
        if rows is None:
            o_ref[:, LANES * p:LANES * (p + 1)] = val
        else:
            o_ref[rows, LANES * p:LANES * (p + 1)] = val


def _softmax_attn_kernel(*refs, k_cols, has_bias):
    if has_bias:
        q_ref, k_ref, v_ref, cum_ref, g_ref, o_ref, m_ref, l_ref, acc_ref = refs
    else:
        q_ref, k_ref, v_ref, g_ref, o_ref, m_ref, l_ref, acc_ref = refs
        cum_ref = None
    tq = q_ref.shape[0]
    tk = tq
    i = pl.program_id(1)
    m_ref[...] = jnp.full(m_ref.shape, NEG_INF, F32)
    l_ref[...] = jnp.zeros(l_ref.shape, F32)
    acc_ref[...] = jnp.zeros(acc_ref.shape, F32)

    half = tq // 2

    def scores(h, rows, key0, nk, kb, cum_cols):
        kc = k_cols[h]
        s = _nt_dot(q_ref[rows, LANES * h:LANES * (h + 1)], k_ref[pl.ds(key0, nk), kc:kc + LANES])
        if has_bias:
            s = s - cum_ref[0, kb][h:h + 1, cum_cols]
        return s

    def update(h, rows, s, key0):
        nk = s.shape[1]
        vc = LANES * (h // 2)
        m_prev = m_ref[h, rows, :]
        m_next = jnp.maximum(m_prev, jnp.max(s, axis=-1, keepdims=True))
        alpha = jnp.exp2(m_prev - m_next)
        p = jnp.exp2(s - jnp.concatenate([m_next] * (nk // LANES), axis=1))
        l_ref[h, rows, :] = alpha * l_ref[h, rows, :] + jnp.sum(p, axis=-1, keepdims=True)
        m_ref[h, rows, :] = m_next
        acc_ref[h, rows, :] = (alpha * acc_ref[h, rows, :]
                               + _dot(p.astype(BF16), v_ref[pl.ds(key0, nk), vc:vc + LANES]))

    every = slice(0, tq)
    lower = slice(half, tq)

    def body2(kb2, carry):
        r0 = pl.multiple_of(kb2 * (2 * tk), 2 * tk)
        for h in range(GROUP_HEADS):
            s = jnp.concatenate([scores(h, every, r0, tk, 2 * kb2, slice(0, tk)),
                                 scores(h, every, r0 + tk, tk, 2 * kb2 + 1, slice(0, tk))], axis=1)
            update(h, every, s, r0)
        return carry

    lax.fori_loop(0, i // 2, body2, 0)

    @pl.when(i % 2 == 1)
    def _():
        r0 = pl.multiple_of((i - 1) * tk, tk)
        for h in range(GROUP_HEADS):
            update(h, every, scores(h, every, r0, tk, i - 1, slice(0, tk)), r0)

    r0 = pl.multiple_of(i * tk, tk)
    r1 = pl.multiple_of(i * tk + half, half)
    causal = (lax.broadcasted_iota(jnp.int32, (half, half), 1)
              <= lax.broadcasted_iota(jnp.int32, (half, half), 0))
    for h in range(GROUP_HEADS):
        s = scores(h, every, r0, half, i, slice(0, half))
        s = jnp.concatenate([jnp.where(causal, s[:half], NEG_INF), s[half:]], axis=0)
        update(h, every, s, r0)
        s = jnp.where(causal, scores(h, lower, r1, half, i, slice(half, tk)), NEG_INF)
        update(h, lower, s, r1)
    heads = [acc_ref[h] / l_ref[h] for h in range(GROUP_HEADS)]
    _pair_norm_store(o_ref, heads, g_ref)


def _softmax_attn(q, k, v, cum, g, *, B, S, k_cols, name):
    M = q.shape[0]
    nq = S // TQ
    kw = k.shape[1]
    has_bias = cum is not None
    in_specs = [
        pl.BlockSpec((TQ, 4 * LANES), lambda b, i: (b * nq + i, 0)),
        pl.BlockSpec((S, kw), lambda b, i: (b, 0)),
        pl.BlockSpec((S, GROUP_WIDTH), lambda b, i: (b, 0)),
    ]
    args = [q, k, v]
    if has_bias:
        in_specs.append(pl.BlockSpec((1, nq, 8, TQ), lambda b, i: (b, 0, 0, 0)))
        args.append(cum)
    in_specs.append(pl.BlockSpec((1, GROUP_WIDTH), lambda b, i: (0, 0)))
    args.append(g)
    kern = functools.partial(_softmax_attn_kernel, k_cols=k_cols, has_bias=has_bias)
    return pl.pallas_call(
        kern, out_shape=jax.ShapeDtypeStruct((M, GROUP_WIDTH), BF16), grid=(B, nq),
        in_specs=in_specs,
        out_specs=pl.BlockSpec((TQ, GROUP_WIDTH), lambda b, i: (b * nq + i, 0)),
        scratch_shapes=[pltpu.VMEM((GROUP_HEADS, TQ, LANES), F32)] * 3,
        compiler_params=pltpu.CompilerParams(dimension_semantics=("arbitrary", "arbitrary"),
                                             vmem_limit_bytes=VMEM_LIMIT),
        name=name,
    )(*args)


def _stick_kernel(q_ref, k_ref, v_ref, tri_ref, g_ref, o_ref, rest_ref, acc_ref):
    tq = q_ref.shape[0]
    tk = tq
    tw = tri_ref.shape[0]
    i = pl.program_id(1)
    rest_ref[...] = jnp.zeros(rest_ref.shape, F32)
    acc_ref[...] = jnp.zeros(acc_ref.shape, F32)

    half = tq // 2
    assert half % tw == 0
    every = slice(0, tq)
    lower = slice(half, tq)
    strict = (lax.broadcasted_iota(jnp.int32, (half, half), 1)
              < lax.broadcasted_iota(jnp.int32, (half, half), 0))

    def mask_rows(x, rows_masked, fill):
        if rows_masked == 0:
            return x
        top = jnp.where(strict, x[:half], fill)
        return top if rows_masked == x.shape[0] else jnp.concatenate([top, x[half:]], axis=0)

    def block(h, rows, key0, nk, rows_masked):
        c0 = LANES * (h // 2)
        tri = tri_ref[...]
        z = _nt_dot(q_ref[rows, LANES * h:LANES * (h + 1)], k_ref[pl.ds(key0, nk), c0:c0 + LANES])
        z_neg = jnp.minimum(z, 0.0)
        lg = jnp.log(1.0 + jnp.exp2(-jnp.abs(z))) * LOG2E
        log_b = z_neg - lg
        log_1mb = mask_rows(log_b - z, rows_masked, 0.0)
        rest = rest_ref[h, rows, :]
        parts = [None] * (nk // tw)
        for c in reversed(range(nk // tw)):
            sub = log_1mb[:, tw * c:tw * (c + 1)]
            parts[c] = (_dot(sub.astype(BF16), tri)
                        + jnp.concatenate([rest] * (tw // LANES), axis=1))
            rest = rest + jnp.sum(sub, axis=-1, keepdims=True)
        rest_ref[h, rows, :] = rest
        a = mask_rows(jnp.exp2(log_b + jnp.concatenate(parts, axis=1)), rows_masked, 0.0)
        acc_ref[h, rows, :] = (acc_ref[h, rows, :]
                               + _dot(a.astype(BF16), v_ref[pl.ds(key0, nk), c0:c0 + LANES]))

    r0 = pl.multiple_of(i * tk, tk)
    r1 = pl.multiple_of(i * tk + half, half)
    for h in range(GROUP_HEADS):
        block(h, lower, r1, half, half)
        block(h, every, r0, half, half)

    def body2(t, carry):
        k0 = pl.multiple_of((i - 2 - 2 * t) * tk, tk)
        for h in range(GROUP_HEADS):
            block(h, every, k0, 2 * tk, 0)
        return carry

    lax.fori_loop(0, i // 2, body2, 0)

    @pl.when(i % 2 == 1)
    def _():
        for h in range(GROUP_HEADS):
            block(h, every, 0, tk, 0)
    _pair_norm_store(o_ref, [acc_ref[h] for h in range(GROUP_HEADS)], g_ref)


def _stick_attn(q, k, v, tri, g, *, B, S):
    M = q.shape[0]
    nq = S // TQ
    return pl.pallas_call(
        _stick_kernel, out_shape=jax.ShapeDtypeStruct((M, GROUP_WIDTH), BF16), grid=(B, nq),
        in_specs=[
            pl.BlockSpec((TQ, 4 * LANES), lambda b, i: (b * nq + i, 0)),
            pl.BlockSpec((S, GROUP_WIDTH), lambda b, i: (b, 0)),
            pl.BlockSpec((S, GROUP_WIDTH), lambda b, i: (b, 0)),
            pl.BlockSpec((TRI_W, TRI_W), lambda b, i: (0, 0)),
            pl.BlockSpec((1, GROUP_WIDTH), lambda b, i: (0, 0)),
        ],
        out_specs=pl.BlockSpec((TQ, GROUP_WIDTH), lambda b, i: (b * nq + i, 0)),
        scratch_shapes=[pltpu.VMEM((GROUP_HEADS, TQ, LANES), F32)] * 2,
        compiler_params=pltpu.CompilerParams(dimension_semantics=("arbitrary", "arbitrary"),
                                             vmem_limit_bytes=VMEM_LIMIT),
        name="stick_attn",
    )(q, k, v, tri, g)


def _swa_kernel(sink_ref, q_ref, kc_ref, kp_ref, vc_ref, vp_ref, g_ref, o_ref, *, slopes):
    tq = q_ref.shape[0]
    w = SWA_WINDOW
    i = pl.program_id(1)
    row = lax.broadcasted_iota(jnp.int32, (w, 2 * w), 0)
    col = lax.broadcasted_iota(jnp.int32, (w, 2 * w), 1)
    dist = row + w - col
    dist_f = dist.astype(F32)
    far = dist >= w
    ahead = dist < 0
    is_prev = col < w
    for r in range(tq // w):
        rows = slice(w * r, w * (r + 1))
        if r == 0:
            k2 = jnp.concatenate([kp_ref[...], kc_ref[0:w, :]], axis=0)
            v2 = jnp.concatenate([vp_ref[...], vc_ref[0:w, :]], axis=0)
        else:
            k2 = kc_ref[w * (r - 1):w * (r + 1), :]
            v2 = vc_ref[w * (r - 1):w * (r + 1), :]
        q4 = jnp.concatenate([q_ref[rows, LANES * s:LANES * (s + 1)] for s in range(GROUP_HEADS)], axis=0)
        s_all = _nt_dot(q4, k2)
        probs, dens = [], []
        for slot in range(GROUP_HEADS):
            head = slot // 2 + 2 * (slot % 2)
            sink = sink_ref[head] * LOG2E
            s = s_all[w * slot:w * (slot + 1), :] - (slopes[head] * LOG2E) * dist_f
            s = jnp.where(far, NEG_INF, jnp.where(ahead, NEG_INF, s))
            if r == 0:
                s = jnp.where(is_prev, jnp.where(i > 0, s, NEG_INF), s)
            m = jnp.maximum(jnp.max(s, axis=-1, keepdims=True), sink)
            p = jnp.exp2(s - m)
            dens.append(jnp.sum(p, axis=-1, keepdims=True) + jnp.exp2(sink - m))
            probs.append(p.astype(BF16))
        pv = _dot(jnp.concatenate(probs, axis=0), v2)
        heads = [pv[w * s:w * (s + 1), :] / dens[s] for s in range(GROUP_HEADS)]
        _pair_norm_store(o_ref, heads, g_ref, rows=rows)


def _swa_attn(sinks, q, k, v, g, *, B, S):
    M = q.shape[0]
    tq = TQ_SWA
    nq = S // tq
    per = tq // SWA_WINDOW
    slopes = tuple(float(2.0 ** (-8.0 * (h + 1) / GROUP_HEADS)) for h in range(GROUP_HEADS))
    cur = lambda b, i: (b * nq + i, 0)
    prev = lambda b, i: (b * nq * per + jnp.maximum(i * per - 1, 0), 0)
    kern = functools.partial(_swa_kernel, slopes=slopes)
    return pl.pallas_call(
        kern, out_shape=jax.ShapeDtypeStruct((M, GROUP_WIDTH), BF16), grid=(B, nq),
        in_specs=[
            pl.BlockSpec(memory_space=pltpu.SMEM),
            pl.BlockSpec((tq, 4 * LANES), cur),
            pl.BlockSpec((tq, LANES), cur),
            pl.BlockSpec((SWA_WINDOW, LANES), prev),
            pl.BlockSpec((tq, LANES), cur),
            pl.BlockSpec((SWA_WINDOW, LANES), prev),
            pl.BlockSpec((1, GROUP_WIDTH), lambda b, i: (0, 0)),
        ],
        out_specs=pl.BlockSpec((tq, GROUP_WIDTH), cur),
        compiler_params=pltpu.CompilerParams(dimension_semantics=("arbitrary", "arbitrary"),
                                             vmem_limit_bytes=VMEM_LIMIT),
        name="swa_attn",
    )(sinks, q, k, k, v, v, g)


def _layernorm(r, g, b):
    mu = jnp.mean(r, axis=-1, keepdims=True)
    d = r - mu
    var = jnp.mean(d * d, axis=-1, keepdims=True)
    return d * lax.rsqrt(var + 1e-5) * g + b


def _out_ffn_kernel(x_ref, ma_ref, mb_ref, mc_ref, md_ref, wo_ref, g1_ref, b1_ref,
                    wg_ref, wu_ref, wd_ref, g2_ref, b2_ref, o_ref, *, alpha):
    y = None
    for n, m_ref in enumerate((ma_ref, mb_ref, mc_ref, md_ref)):
        part = _dot(m_ref[...], wo_ref[GROUP_WIDTH * n:GROUP_WIDTH * (n + 1), :])
        y = part if y is None else y + part
    x1 = _layernorm(alpha * x_ref[...] + y, g1_ref[...], b1_ref[...])
    x1b = x1.astype(BF16)
    f = None
    for c0 in range(0, D_FF, F_CHUNK):
        cols = slice(c0, min(c0 + F_CHUNK, D_FF))
        gate = _dot(x1b, wg_ref[:, cols])
        up = _dot(x1b, wu_ref[:, cols])
        hid = (gate * (1.0 / (1.0 + jnp.exp(-gate))) * up).astype(BF16)
        part = _dot(hid, wd_ref[cols, :])
        f = part if f is None else f + part
    o_ref[...] = _layernorm(alpha * x1 + f, g2_ref[...], b2_ref[...])


def _out_ffn(x2, mixes, wo, g1, b1, wg, wu, wd, g2, b2, *, alpha, layer):
    M = x2.shape[0]
    tm = TM_OUT
    row = lambda i: (i, 0)
    const = lambda i: (0, 0)
    resident = lambda w: pl.BlockSpec((None,) + w.shape[1:], lambda i: (layer, 0, 0),
                                      pipeline_mode=pl.Buffered(1))
    vec = pl.BlockSpec((1, D_MODEL), const)
    in_specs = [pl.BlockSpec((tm, D_MODEL), row)]
    in_specs += [pl.BlockSpec((tm, GROUP_WIDTH), row)] * 4
    in_specs += [resident(wo), vec, vec, resident(wg), resident(wu), resident(wd), vec, vec]
    kern = functools.partial(_out_ffn_kernel, alpha=alpha)
    return pl.pallas_call(
        kern, out_shape=jax.ShapeDtypeStruct((M, D_MODEL), F32), grid=(M // tm,),
        in_specs=in_specs, out_specs=pl.BlockSpec((tm, D_MODEL), row),
        compiler_params=pltpu.CompilerParams(dimension_semantics=("arbitrary",),
                                             vmem_limit_bytes=VMEM_LIMIT),
        name="out_ffn",
    )(x2, *mixes, wo, g1, b1, wg, wu, wd, g2, b2)


def kernel(x, w_in, fox_b_f, mla_g_q, mla_g_kv, mla_w_uq, mla_w_ukv, swa_sinks, mix_g, w_o,
           ln1_g, ln1_b, w_gate, w_up, w_down, ln2_g, ln2_b):
    B, S, D = x.shape
    depth = w_in.shape[0]
    assert D == D_MODEL and S % TM_IN == 0 and S % TQ_SWA == 0 and (B * S) % TM_OUT == 0

    w_in_p = _gather_cols(w_in.astype(BF16), _in_proj_columns())
    wq_p = _gather_cols(mla_w_uq.astype(BF16), _mla_q_columns())
    wkv_p = _gather_cols(mla_w_ukv.astype(BF16), _mla_kv_columns())
    order = _mix_row_order()
    wo_p = _gather_cols(w_o.astype(BF16), order, axis=1)
    mix_g_p = _gather_cols(mix_g, order)
    wg_b, wu_b, wd_b = w_gate.astype(BF16), w_up.astype(BF16), w_down.astype(BF16)
    b_f = jnp.pad(fox_b_f.astype(F32), ((0, 0), (0, LANES - fox_b_f.shape[1])))
    tc, ts = _rope_slot_tables(S)
    idx_in = np.arange(TM_IN)
    tri_incl = jnp.asarray(idx_in[:, None] <= idx_in[None, :], BF16)
    idx_q = np.arange(TRI_W)
    tri_after = jnp.asarray(idx_q[:, None] > idx_q[None, :], BF16)

    k_cols_packed = tuple(LANES * (h // 2) for h in range(GROUP_HEADS))
    k_cols_slots = tuple(LANES * h for h in range(GROUP_HEADS))

    x2 = x.reshape(B * S, D)
    for l in range(depth):
        (qa, ka, va, cum, qb, kb, vb, qc, kc, vc, qd, kd, vd) = _in_proj(
            x2, w_in_p, wq_p, wkv_p, mla_g_q[l][None, :], mla_g_kv[l][None, :],
            b_f[l][None, :], tc, ts, tri_incl, B=B, S=S, layer=l)
        g = mix_g_p[l][None, :]
        mix_a = _softmax_attn(qa, ka, va, cum, g[:, 0:256], B=B, S=S, k_cols=k_cols_packed, name="fox_attn")
        mix_b = _softmax_attn(qb, kb, vb, None, g[:, 256:512], B=B, S=S, k_cols=k_cols_slots, name="mla_attn")
        mix_c = _stick_attn(qc, kc, vc, tri_after, g[:, 512:768], B=B, S=S)
        mix_d = _swa_attn(swa_sinks[l].astype(F32), qd, kd, vd, g[:, 768:1024], B=B, S=S)
        x2 = _out_ffn(x2, (mix_a, mix_b, mix_c, mix_d), wo_p, ln1_g[l][None, :], ln1_b[l][None, :],
                      wg_b, wu_b, wd_b, ln2_g[l][None, :], ln2_b[l][None, :], alpha=ALPHA, layer=l)
    return x2.reshape(B, S, D)
```
